```python
import math
import jax, jax.numpy as jnp
from jax import lax
import numpy as np

D_MODEL = 1024
BATCH = 4
SEQ = 8192
DEPTH = 2
DEC_BATCH = 32
DEC_SEQ = 4
PAST_LEN = 16384
PAGE_SIZE = 128

N_EVEN = (DEPTH + 1) // 2
N_ODD = DEPTH // 2
PLE_DIM = 256
EPS = 1e-6
RET_HEADS = 4
RET_QK_DIM = 128
RET_V_DIM = 128
RET_QK_WIDTH = RET_HEADS * RET_QK_DIM
RET_WIDTH = RET_HEADS * RET_V_DIM
RET_CHUNK = 128
ROPE_THETA = 10000.0
LRU_WIDTH = D_MODEL // 2
LRU_BLOCKS = 8
LRU_BLOCK_DIM = LRU_WIDTH // LRU_BLOCKS
LRU_CONV = 4
LRU_C = 8.0
MIX0_SPLITS = (RET_QK_WIDTH, RET_QK_WIDTH, RET_WIDTH, RET_WIDTH, LRU_WIDTH, LRU_WIDTH)
MIX0_IN = sum(MIX0_SPLITS)
MIX0_OUT = RET_WIDTH + LRU_WIDTH
C_HEADS = 16
C_HEAD_DIM = 64
C_WIDTH = C_HEADS * C_HEAD_DIM
MOBA_BLOCK = 256
MOBA_TOPK = 3
MOBA_Q_BLOCK = 16
REL_BUCKETS = 32
REL_MAX_DIST = 128
D_FF = 3 * D_MODEL
FFN_CONV = 3

kernel_name = 'hybrid_retention_rglru_moba_step'


def rmsnorm(x, g):
    xf = x.astype(jnp.float32)
    y = xf * lax.rsqrt(jnp.mean(xf * xf, axis=-1, keepdims=True) + EPS)
    return (y * g.astype(jnp.float32)).astype(x.dtype)


def head_rms(x, g):
    xf = x.astype(jnp.float32)
    return xf * lax.rsqrt(jnp.mean(xf * xf, axis=-1, keepdims=True) + EPS) * g.astype(jnp.float32)


def causal_dwconv(x, prev, w, b):
    k = w.shape[0]
    t = x.shape[1]
    xp = jnp.concatenate([prev.astype(x.dtype), x], axis=1)
    y = b + sum(xp[:, j:j + t] * w[j] for j in range(k))
    return y, xp[:, xp.shape[1] - (k - 1):]


def rope(x, pos):
    half = x.shape[-1] // 2
    inv = ROPE_THETA ** (-jnp.arange(half, dtype=jnp.float32) / half)
    ang = pos.astype(jnp.float32)[:, None] * inv[None, :]
    cos = jnp.cos(ang)[None, :, None, :]
    sin = jnp.sin(ang)[None, :, None, :]
    x1, x2 = x[..., :half], x[..., half:]
    return jnp.concatenate([x1 * cos - x2 * sin, x1 * sin + x2 * cos], axis=-1)


def retention(q, k, v, s0):
    b, t, h, _ = q.shape
    dv = v.shape[-1]
    c = min(RET_CHUNK, t)
    n = t // c
    log_g = jnp.log1p(-jnp.exp2(-5.0 - jnp.arange(h, dtype=jnp.float32)))
    idx = jnp.arange(c, dtype=jnp.float32)
    diff = idx[:, None] - idx[None, :]
    dmask = jnp.where(diff >= 0, jnp.exp(log_g[:, None, None] * jnp.maximum(diff, 0.0)), 0.0)
    q_dec = jnp.exp(log_g[None, :] * (idx[:, None] + 1.0))
    k_dec = jnp.exp(log_g[None, :] * (c - 1.0 - idx[:, None]))
    c_dec = jnp.exp(log_g * c)

    def to_chunks(a):
        return jnp.moveaxis(a.reshape(b, n, c, h, a.shape[-1]), 1, 0)

    def step(s, xs):
        qc, kc, vc = xs
        att = jnp.einsum('bihd,bjhd->bhij', qc, kc) * dmask[None]
        o = (jnp.einsum('bhij,bjhe->bihe', att, vc)
             + jnp.einsum('bihd,bhde->bihe', qc, s) * q_dec[None, :, :, None])
        s = s * c_dec[None, :, None, None] + jnp.einsum('bjhd,bjhe,jh->bhde', kc, vc, k_dec)
        return s, o

    s, o = lax.scan(step, s0, (to_chunks(q), to_chunks(k), to_chunks(v)))
    return jnp.moveaxis(o, 0, 1).reshape(b, t, h, dv), s


def rglru(xc, h0, w_a, b_a, w_i, b_i, lam):
    xf = xc.astype(jnp.float32)
    xb = xf.reshape(xf.shape[:-1] + (LRU_BLOCKS, LRU_BLOCK_DIM))
    r = jax.nn.sigmoid(jnp.einsum('btnc,ncd->btnd', xb, w_a).reshape(xf.shape) + b_a)
    i = jax.nn.sigmoid(jnp.einsum('btnc,ncd->btnd', xb, w_i).reshape(xf.shape) + b_i)
    log_a = -LRU_C * r * jax.nn.softplus(-lam.astype(jnp.float32))
    a = jnp.exp(log_a)
    u = jnp.sqrt(-jnp.expm1(2.0 * log_a)) * (i * xf)
    u = u.at[:, 0].add(a[:, 0] * h0.astype(jnp.float32))

    def comb(e1, e2):
        return e1[0] * e2[0], e2[0] * e1[1] + e2[1]

    _, hs = lax.associative_scan(comb, (a, u), axis=1)
    return hs, hs[:, -1]


def mix_ret_lru(xn, pos, s0, h0, conv0, w_in, gn_g, gn_b, conv_w, conv_b, w_a, b_a, w_i, b_i, lam, w_out):
    b, t, _ = xn.shape
    bounds = [int(c) for c in np.cumsum(MIX0_SPLITS)[:-1]]
    q, k, v, g, bx, bg = jnp.split(xn @ w_in, bounds, axis=-1)
    q = rope(q.reshape(b, t, RET_HEADS, RET_QK_DIM).astype(jnp.float32), pos)
    k = rope(k.reshape(b, t, RET_HEADS, RET_QK_DIM).astype(jnp.float32), pos) * (RET_QK_DIM ** -0.5)
    v = v.reshape(b, t, RET_HEADS, RET_V_DIM).astype(jnp.float32)
    o, s_new = retention(q, k, v, s0.astype(jnp.float32))
    mu = jnp.mean(o, axis=-1, keepdims=True)
    var = jnp.mean(jnp.square(o - mu), axis=-1, keepdims=True)
    o = ((o - mu) * lax.rsqrt(var + EPS)).reshape(b, t, RET_WIDTH) * gn_g + gn_b
    o = o * jax.nn.silu(g.astype(jnp.float32))
    xc, conv_new = causal_dwconv(bx, conv0, conv_w, conv_b)
    hs, h_last = rglru(xc, h0, w_a, b_a, w_i, b_i, lam)
    ob = hs * jax.nn.gelu(bg.astype(jnp.float32), approximate=True)
    y = jnp.concatenate([o, ob], axis=-1).astype(xn.dtype) @ w_out
    return y, s_new.astype(s0.dtype), h_last.astype(h0.dtype), conv_new.astype(conv0.dtype)


def t5_bucket(dist):
    n = jnp.maximum(dist, 0)
    max_exact = REL_BUCKETS // 2
    nf = jnp.maximum(n, max_exact).astype(jnp.float32)
    large = max_exact + (jnp.log(nf / max_exact) / math.log(REL_MAX_DIST / max_exact)
                         * (REL_BUCKETS - max_exact)).astype(jnp.int32)
    large = jnp.minimum(large, REL_BUCKETS - 1)
    return jnp.where(n < max_exact, n, large)


def moba_attend(q, q_pos, ko, vo, ko_pos, rel_bias, sel):
    h, dh = q.shape[2], q.shape[3]
    scale = dh ** -0.5
    tbl = rel_bias.astype(jnp.float32).T
    lo = jnp.einsum('bqhd,blhd->bqhl', q, ko) * scale
    d_o = q_pos[:, None] - ko_pos[None, :]
    lo = lo + jnp.transpose(tbl[:, t5_bucket(d_o)], (1, 0, 2))[None]
    lo = jnp.where((d_o >= 0)[None, :, None, :], lo, -jnp.inf)
    if sel is None:
        return jnp.einsum('bqhl,blhd->bqhd', jax.nn.softmax(lo, axis=-1), vo)
    ks, vs, ks_pos, slot_ok = sel
    n, lb = ks.shape[3], ks.shape[4]
    ls = jnp.einsum('bqhd,bqhnld->bqhnl', q, ks) * scale
    d_s = q_pos[None, :, None, None, None] - ks_pos
    ls = ls + tbl[jnp.arange(h)[:, None, None], t5_bucket(d_s)]
    ls = jnp.where(slot_ok[:, None], ls, -jnp.inf)
    logits = jnp.concatenate([ls.reshape(ls.shape[:3] + (n * lb,)), lo], axis=-1)
    p = jax.nn.softmax(logits, axis=-1)
    ps = p[..., :n * lb].reshape(ls.shape)
    po = p[..., n * lb:]
    return jnp.einsum('bqhnl,bqhnld->bqhd', ps, vs) + jnp.einsum('bqhl,blhd->bqhd', po, vo)


def moba_prompt(q, k, v, rel_bias):
    b, t, h, dh = q.shape
    nb_full = t // MOBA_BLOCK
    nb_all = -(-t // MOBA_BLOCK)
    pad = ((0, 0), (0, nb_all * MOBA_BLOCK - t), (0, 0), (0, 0))
    kp = jnp.pad(k, pad)
    vp = jnp.pad(v, pad)
    kb = jnp.transpose(kp.reshape(b, nb_all, MOBA_BLOCK, h, dh), (0, 3, 1, 2, 4))
    vb = jnp.transpose(vp.reshape(b, nb_all, MOBA_BLOCK, h, dh), (0, 3, 1, 2, 4))
    kmean = jnp.mean(kb[:, :, :nb_full], axis=3)
    n_sel = min(MOBA_TOPK, nb_full)
    blk = jnp.arange(MOBA_BLOCK)
    bi = jnp.arange(b)[:, None, None, None]
    hi = jnp.arange(h)[None, None, :, None]

    def one_chunk(ci):
        t0 = ci * MOBA_Q_BLOCK
        qc = lax.dynamic_slice_in_dim(q, t0, MOBA_Q_BLOCK, axis=1)
        q_pos = t0 + jnp.arange(MOBA_Q_BLOCK)
        own = t0 // MOBA_BLOCK
        ko = lax.dynamic_slice_in_dim(kp, own * MOBA_BLOCK, MOBA_BLOCK, axis=1)
        vo = lax.dynamic_slice_in_dim(vp, own * MOBA_BLOCK, MOBA_BLOCK, axis=1)
        ko_pos = own * MOBA_BLOCK + blk
        sel = None
        if n_sel > 0:
            sc = jnp.einsum('bqhd,bhnd->bqhn', qc, kmean)
            sc = jnp.where(jnp.arange(nb_full) < own, sc, -jnp.inf)
            _, idx = lax.top_k(sc, n_sel)
            sel = (kb[bi, hi, idx], vb[bi, hi, idx],
                   idx[..., None] * MOBA_BLOCK + blk, jnp.arange(n_sel) < own)
        return moba_attend(qc, q_pos, ko, vo, ko_pos, rel_bias, sel)

    out = lax.map(one_chunk, jnp.arange(t // MOBA_Q_BLOCK))
    return jnp.moveaxis(out, 0, 1).reshape(b, t, h, dh)


def moba_sample(q, k_new, v_new, ck, cv, page_table, rel_bias):
    b, s, h, dh = q.shape
    ppb = MOBA_BLOCK // PAGE_SIZE
    own = PAST_LEN // MOBA_BLOCK
    r0 = PAST_LEN - own * MOBA_BLOCK
    q_pos = PAST_LEN + jnp.arange(s)
    own_pages = page_table[:, own * ppb: PAST_LEN // PAGE_SIZE]
    ko = jnp.concatenate([ck[own_pages].reshape(b, r0, h, dh).astype(jnp.float32), k_new], axis=1)
    vo = jnp.concatenate([cv[own_pages].reshape(b, r0, h, dh).astype(jnp.float32), v_new], axis=1)
    ko_pos = jnp.concatenate([own * MOBA_BLOCK + jnp.arange(r0), q_pos])
    n_sel = min(MOBA_TOPK, own)
    sel = None
    if n_sel > 0:
        kmean = jnp.mean(ck[page_table[:, :own * ppb]].astype(jnp.float32)
                         .reshape(b, own, MOBA_BLOCK, h, dh), axis=2)
        sc = jnp.einsum('bqhd,bnhd->bqhn', q, kmean)
        _, idx = lax.top_k(sc, n_sel)
        bi = jnp.arange(b)[:, None, None, None, None]
        hi = jnp.arange(h)[None, None, :, None, None]
        phys = page_table[bi, idx[..., None] * ppb + jnp.arange(ppb)]
        gshape = idx.shape + (MOBA_BLOCK, dh)
        ks = ck[phys, :, hi].reshape(gshape).astype(jnp.float32)
        vs = cv[phys, :, hi].reshape(gshape).astype(jnp.float32)
        sel = (ks, vs, idx[..., None] * MOBA_BLOCK + jnp.arange(MOBA_BLOCK), jnp.ones((n_sel,), bool))
    return moba_attend(q, q_pos, ko, vo, ko_pos, rel_bias, sel)


def run_trunk(x, p, pos, ret0, lru0, lconv0, fconv0, attend, w):
    h = x
    b, t, _ = x.shape
    rets, lrus, lconvs, fconvs, ks, vs = [], [], [], [], [], []
    for l in range(DEPTH):
        xn = rmsnorm(h, w['norm_mix'][l])
        if l % 2 == 0:
            e = l // 2
            y, s_new, h_last, lc_new = mix_ret_lru(
                xn, pos, ret0[e], lru0[e], lconv0[e], w['w0_in'][e], w['ret_gn_g'][e], w['ret_gn_b'][e],
                w['lru_conv_w'][e], w['lru_conv_b'][e], w['lru_wa'][e], w['lru_ba'][e],
                w['lru_wi'][e], w['lru_bi'][e], w['lru_lam'][e], w['w0_out'][e])
            rets.append(s_new)
            lrus.append(h_last)
            lconvs.append(lc_new)
        else:
            o = l // 2
            q, k, v = jnp.split(xn @ w['c_in'][o], 3, axis=-1)
            q = head_rms(q.reshape(b, t, C_HEADS, C_HEAD_DIM), w['c_qn'][o])
            k = head_rms(k.reshape(b, t, C_HEADS, C_HEAD_DIM), w['c_kn'][o])
            v = v.reshape(b, t, C_HEADS, C_HEAD_DIM).astype(jnp.float32)
            att = attend(o, q, k, v)
            y = att.reshape(b, t, C_WIDTH).astype(h.dtype) @ w['c_out'][o]
            ks.append(k.astype(h.dtype))
            vs.append(v.astype(h.dtype))
        h = h + y
        xn = rmsnorm(h, w['norm_ffn'][l])
        g, u = jnp.split(xn @ w['ffn_in'][l], 2, axis=-1)
        gc, fc_new = causal_dwconv(g, fconv0[l], w['ffn_conv_w'][l], w['ffn_conv_b'][l])
        h = h + (jax.nn.gelu(gc, approximate=True) * u) @ w['ffn_out'][l]
        fconvs.append(fc_new.astype(fconv0.dtype))
        e_in = p[l] @ w['ple_proj'][l]
        gate = jax.nn.sigmoid(rmsnorm(h, w['ple_gate_norm'][l]) @ w['ple_gate'][l])
        h = h + rmsnorm(e_in, w['ple_post_norm'][l]) * gate
    return (h, jnp.stack(rets), jnp.stack(lrus), jnp.stack(lconvs), jnp.stack(fconvs),
            jnp.stack(ks), jnp.stack(vs))


def setup_inputs(seed: int = 0) -> dict:
    key = jax.random.key(seed)
    keys = jax.random.split(key, 64)
    cnt = [0]

    def nk():
        cnt[0] += 1
        return keys[cnt[0] - 1]

    def nrm(shape, scale):
        return scale * jax.random.normal(nk(), shape, jnp.float32)

    def gain(shape):
        return 1.0 + nrm(shape, 0.02)

    n_pages = PAST_LEN // PAGE_SIZE
    n_used = DEC_BATCH * n_pages
    n_pool = n_used + n_used // 4
    page_table = jax.random.permutation(nk(), n_pool)[:n_used].reshape(DEC_BATCH, n_pages).astype(jnp.int32)
    ua = jax.random.uniform(nk(), (N_EVEN, LRU_WIDTH), jnp.float32, 0.9, 0.999)
    a = ua ** (1.0 / LRU_C)
    lam = jnp.log(a) - jnp.log1p(-a)
    return {
        'x_prompt': nrm((BATCH, SEQ, D_MODEL), 1.0),
        'x_sample': nrm((DEC_BATCH, DEC_SEQ, D_MODEL), 1.0),
        'state_ret': nrm((N_EVEN, DEC_BATCH, RET_HEADS, RET_QK_DIM, RET_V_DIM), 0.1),
        'state_lru': nrm((N_EVEN, DEC_BATCH, LRU_WIDTH), 0.5),
        'state_lru_conv': nrm((N_EVEN, DEC_BATCH, LRU_CONV - 1, LRU_WIDTH), 1.0),
        'state_ffn_conv': nrm((DEPTH, DEC_BATCH, FFN_CONV - 1, D_FF), 1.0),
        'cache_k': nrm((N_ODD, n_pool, PAGE_SIZE, C_HEADS, C_HEAD_DIM), 1.0),
        'cache_v': nrm((N_ODD, n_pool, PAGE_SIZE, C_HEADS, C_HEAD_DIM), 1.0),
        'page_table': page_table,
        'p_prompt': nrm((DEPTH, BATCH, SEQ, PLE_DIM), 1.0),
        'p_sample': nrm((DEPTH, DEC_BATCH, DEC_SEQ, PLE_DIM), 1.0),
        'norm_mix': gain((DEPTH, D_MODEL)),
        'norm_ffn': gain((DEPTH, D_MODEL)),
        'w0_in': nrm((N_EVEN, D_MODEL, MIX0_IN), D_MODEL ** -0.5),
        'ret_gn_g': gain((N_EVEN, RET_WIDTH)),
        'ret_gn_b': nrm((N_EVEN, RET_WIDTH), 0.02),
        'lru_conv_w': nrm((N_EVEN, LRU_CONV, LRU_WIDTH), LRU_CONV ** -0.5),
        'lru_conv_b': nrm((N_EVEN, LRU_WIDTH), 0.02),
        'lru_wa': nrm((N_EVEN, LRU_BLOCKS, LRU_BLOCK_DIM, LRU_BLOCK_DIM), LRU_BLOCK_DIM ** -0.5),
        'lru_ba': nrm((N_EVEN, LRU_WIDTH), 0.02),
        'lru_wi': nrm((N_EVEN, LRU_BLOCKS, LRU_BLOCK_DIM, LRU_BLOCK_DIM), LRU_BLOCK_DIM ** -0.5),
        'lru_bi': nrm((N_EVEN, LRU_WIDTH), 0.02),
        'lru_lam': lam,
        'w0_out': nrm((N_EVEN, MIX0_OUT, D_MODEL), MIX0_OUT ** -0.5),
        'c_in': nrm((N_ODD, D_MODEL, 3 * C_WIDTH), D_MODEL ** -0.5),
        'c_qn': gain((N_ODD, C_HEAD_DIM)),
        'c_kn': gain((N_ODD, C_HEAD_DIM)),
        'c_out': nrm((N_ODD, C_WIDTH, D_MODEL), C_WIDTH ** -0.5),
        'rel_bias': nrm((REL_BUCKETS, C_HEADS), 0.5),
        'ffn_in': nrm((DEPTH, D_MODEL, 2 * D_FF), D_MODEL ** -0.5),
        'ffn_conv_w': nrm((DEPTH, FFN_CONV, D_FF), FFN_CONV ** -0.5),
        'ffn_conv_b': nrm((DEPTH, D_FF), 0.02),
        'ffn_out': nrm((DEPTH, D_FF, D_MODEL), D_FF ** -0.5),
        'ple_proj': nrm((DEPTH, PLE_DIM, D_MODEL), PLE_DIM ** -0.5),
        'ple_gate_norm': gain((DEPTH, D_MODEL)),
        'ple_gate': nrm((DEPTH, D_MODEL, D_MODEL), D_MODEL ** -0.5),
        'ple_post_norm': gain((DEPTH, D_MODEL)),
    }


def reference(x_prompt, x_sample, state_ret, state_lru, state_lru_conv, state_ffn_conv, cache_k, cache_v,
              page_table, p_prompt, p_sample, norm_mix, norm_ffn, w0_in, ret_gn_g, ret_gn_b, lru_conv_w,
              lru_conv_b, lru_wa, lru_ba, lru_wi, lru_bi, lru_lam, w0_out, c_in, c_qn, c_kn, c_out, rel_bias,
              ffn_in, ffn_conv_w, ffn_conv_b, ffn_out, ple_proj, ple_gate_norm, ple_gate, ple_post_norm):
    w = dict(norm_mix=norm_mix, norm_ffn=norm_ffn, w0_in=w0_in, ret_gn_g=ret_gn_g, ret_gn_b=ret_gn_b,
             lru_conv_w=lru_conv_w, lru_conv_b=lru_conv_b, lru_wa=lru_wa, lru_ba=lru_ba, lru_wi=lru_wi,
             lru_bi=lru_bi, lru_lam=lru_lam, w0_out=w0_out, c_in=c_in, c_qn=c_qn, c_kn=c_kn, c_out=c_out,
             ffn_in=ffn_in, ffn_conv_w=ffn_conv_w, ffn_conv_b=ffn_conv_b, ffn_out=ffn_out, ple_proj=ple_proj,
             ple_gate_norm=ple_gate_norm, ple_gate=ple_gate, ple_post_norm=ple_post_norm)
    bp, tp, _ = x_prompt.shape
    ret0 = jnp.zeros((N_EVEN, bp, RET_HEADS, RET_QK_DIM, RET_V_DIM), jnp.float32)
    lru0 = jnp.zeros((N_EVEN, bp, LRU_WIDTH), jnp.float32)
    lconv0 = jnp.zeros((N_EVEN, bp, LRU_CONV - 1, LRU_WIDTH), jnp.float32)
    fconv0 = jnp.zeros((DEPTH, bp, FFN_CONV - 1, D_FF), jnp.float32)

    def attend_prompt(o, q, k, v):
        return moba_prompt(q, k, v, rel_bias)

    def attend_sample(o, q, k, v):
        return moba_sample(q, k, v, cache_k[o], cache_v[o], page_table, rel_bias)

    pos_p = jnp.arange(tp)
    pos_s = PAST_LEN + jnp.arange(x_sample.shape[1])
    y_prompt, ret_p, lru_p, lconv_p, fconv_p, k_p, v_p = run_trunk(
        x_prompt, p_prompt, pos_p, ret0, lru0, lconv0, fconv0, attend_prompt, w)
    y_sample, ret_s, lru_s, lconv_s, fconv_s, k_s, v_s = run_trunk(
        x_sample, p_sample, pos_s, state_ret, state_lru, state_lru_conv, state_ffn_conv, attend_sample, w)
    return (y_prompt, y_sample, ret_p, ret_s, lru_p, lru_s, lconv_p, lconv_s, fconv_p, fconv_s, k_p, k_s, v_p, v_s)
```

```python
import functools
import math

import numpy as np
import jax
import jax.numpy as jnp
from jax import lax
from jax.experimental import pallas as pl
from jax.experimental.pallas import tpu as pltpu

F32 = jnp.float32
BF16 = jnp.bfloat16
EPS = 1e-6
NEG = -1e30

LANES = 128
SUBLANES = 8
VMEM_LIMIT_BYTES = 48 * 1024 * 1024

RET_HEADS = 4
RET_DIM = 128
RET_CHUNK = 128
ROPE_THETA = 10000.0
LRU_C = 8.0
LRU_CONV = 4
FFN_CONV = 3
C_HEADS = 16
C_HEAD_DIM = 64
MOBA_BLOCK = 256
MOBA_TOPK = 3
REL_BUCKETS = 32
REL_MAX_DIST = 128


def _params(*sem):
    return pltpu.CompilerParams(dimension_semantics=sem, vmem_limit_bytes=VMEM_LIMIT_BYTES)


def _rms(x, g):
    return x * lax.rsqrt(jnp.mean(x * x, axis=-1, keepdims=True) + EPS) * g


def _gelu(x):
    return 0.5 * x * (1.0 + jnp.tanh(0.7978845608028654 * (x + 0.044715 * (x * x * x))))


def _dot(a, b):
    return jnp.dot(a, b, preferred_element_type=F32)


def _dot_nt(a, b):
    return lax.dot_general(a, b, (((1,), (1,)), ((), ())), preferred_element_type=F32)


def _dot_tn(a, b):
    return lax.dot_general(a, b, (((0,), (0,)), ((), ())), preferred_element_type=F32)


def _head_rms(y, hd):
    tm, tn = y.shape
    lo = lax.broadcasted_iota(jnp.int32, (tm, LANES), 1) < hd
    outs = []
    for s in range(tn // LANES):
        ys = y[:, s * LANES:(s + 1) * LANES]
        y2 = ys * ys
        s0 = jnp.sum(jnp.where(lo, y2, 0.0), axis=-1, keepdims=True)
        s1 = jnp.sum(jnp.where(lo, 0.0, y2), axis=-1, keepdims=True)
        r = lax.rsqrt(jnp.where(lo, s0, s1) * (1.0 / hd) + EPS)
        outs.append(ys * r)
    return jnp.concatenate(outs, axis=-1)


def _norm_proj_kernel(x_ref, g_ref, w_ref, hg_ref, *rest, n_rms_tiles, n_out):
    o_refs, xn_ref = rest[:n_out], rest[n_out]
    j = pl.program_id(1)

    @pl.when(j == 0)
    def _():
        xn_ref[...] = _rms(x_ref[...], g_ref[...]).astype(BF16)

    y = _dot(xn_ref[...], w_ref[...])
    if n_out == 1:
        o_refs[0][...] = y
    else:
        for t in range(n_out):
            @pl.when(j == t)
            def _(t=t):
                if t < n_rms_tiles:
                    o_refs[t][...] = _head_rms(y, C_HEAD_DIM) * hg_ref[...]
                else:
                    o_refs[t][...] = y


def _norm_proj(x, g, w_bf16, *, tm, tn, head_gain=None, n_rms_tiles=0, split=False):
    n, d = x.shape
    nout = w_bf16.shape[1]
    nj = nout // tn
    if head_gain is None:
        head_gain = jnp.ones((1, nout), F32)
    if split:
        out_shape = [jax.ShapeDtypeStruct((n, tn), F32) for _ in range(nj)]
        out_specs = [pl.BlockSpec((tm, tn), lambda i, j: (i, 0)) for _ in range(nj)]
        n_out = nj
    else:
        out_shape = [jax.ShapeDtypeStruct((n, nout), F32)]
        out_specs = [pl.BlockSpec((tm, tn), lambda i, j: (i, j))]
        n_out = 1
    outs = pl.pallas_call(
        functools.partial(_norm_proj_kernel, n_rms_tiles=n_rms_tiles, n_out=n_out),
        grid=(n // tm, nj),
        in_specs=[
            pl.BlockSpec((tm, d), lambda i, j: (i, 0)),
            pl.BlockSpec((1, d), lambda i, j: (0, 0)),
            pl.BlockSpec((d, tn), lambda i, j: (0, j)),
            pl.BlockSpec((1, tn), lambda i, j: (0, j)),
        ],
        out_specs=out_specs,
        out_shape=out_shape,
        scratch_shapes=[pltpu.VMEM((tm, d), BF16)],
        compiler_params=_params("arbitrary", "arbitrary"),
        name="norm_proj",
    )(x, g.reshape(1, d), w_bf16, head_gain)
    return outs if split else outs[0]


def _ret_kernel(q_ref, k_ref, v_ref, g_ref, s0_ref, cos_ref, sin_ref, dm_ref, qd_ref, kd_ref, cd_ref,
                gg_ref, gb_ref, o_ref, s_ref, st_ref, *, c, nsub):
    t = pl.program_id(2)

    @pl.when(t == 0)
    def _():
        st_ref[...] = s0_ref[0, 0]

    dm = dm_ref[0]
    qd = qd_ref[0]
    kd = kd_ref[0]
    cd = cd_ref[0]
    gg = gg_ref[...]
    gb = gb_ref[...]
    for u in range(nsub):
        rows = slice(u * c, (u + 1) * c)
        cos = cos_ref[rows, :]
        sin = sin_ref[rows, :]
        q = q_ref[0, rows, :]
        k = k_ref[0, rows, :]
        q = q * cos + pltpu.roll(q, RET_DIM // 2, 1) * sin
        k = (k * cos + pltpu.roll(k, RET_DIM // 2, 1) * sin) * (RET_DIM ** -0.5)
        s = st_ref[...]
        qb = q.astype(BF16)
        vb = v_ref[0, rows, :].astype(BF16)
        att = _dot_nt(qb, k.astype(BF16)) * dm
        o = _dot(att.astype(BF16), vb) + _dot(qb, s.astype(BF16)) * qd
        st_ref[...] = s * cd + _dot_tn((k * kd).astype(BF16), vb)
        mu = jnp.mean(o, axis=-1, keepdims=True)
        oc = o - mu
        var = jnp.mean(oc * oc, axis=-1, keepdims=True)
        o = oc * lax.rsqrt(var + EPS) * gg + gb
        gate = g_ref[0, rows, :]
        o_ref[0, rows, :] = o * (gate * jax.nn.sigmoid(gate))

    @pl.when(t == pl.num_programs(2) - 1)
    def _():
        s_ref[0, 0] = st_ref[...]


def _retention(proj, s0, pos, gn_g, gn_b, *, c_eff, c_pad, rows_per_step):
    b, t, _ = proj.shape
    h = RET_HEADS
    half = RET_DIM // 2
    inv = ROPE_THETA ** (-jnp.arange(half, dtype=F32) / half)
    ang = pos.astype(F32)[:, None] * inv[None, :]
    cos2 = jnp.concatenate([jnp.cos(ang), jnp.cos(ang)], axis=-1)
    sin2 = jnp.concatenate([-jnp.sin(ang), jnp.sin(ang)], axis=-1)
    log_g = jnp.log1p(-jnp.exp2(-5.0 - jnp.arange(h, dtype=F32)))
    idx = jnp.arange(c_pad, dtype=F32)
    diff = idx[:, None] - idx[None, :]
    dmask = jnp.where(diff >= 0, jnp.exp(log_g[:, None, None] * jnp.maximum(diff, 0.0)), 0.0)
    q_dec = jnp.exp(log_g[:, None, None] * (idx[None, :, None] + 1.0))
    k_dec = jnp.exp(log_g[:, None, None] * (c_eff - 1.0 - idx[None, :, None]))
    c_dec = jnp.exp(log_g * c_eff).reshape(h, 1, 1)
    r = rows_per_step
    nsub = r // c_pad
    col = lambda off: pl.BlockSpec((1, r, RET_DIM), lambda bi, hi, ti, off=off: (bi, ti, off + hi))
    per_head = lambda shape: pl.BlockSpec((1,) + shape, lambda bi, hi, ti: (hi, 0, 0))
    o, s_new = pl.pallas_call(
        functools.partial(_ret_kernel, c=c_pad, nsub=nsub),
        grid=(b, h, t // r),
        in_specs=[
            col(0), col(h), col(2 * h), col(3 * h),
            pl.BlockSpec((1, 1, RET_DIM, RET_DIM), lambda bi, hi, ti: (bi, hi, 0, 0)),
            pl.BlockSpec((r, RET_DIM), lambda bi, hi, ti: (ti, 0)),
            pl.BlockSpec((r, RET_DIM), lambda bi, hi, ti: (ti, 0)),
            per_head((c_pad, c_pad)), per_head((c_pad, 1)), per_head((c_pad, 1)), per_head((1, 1)),
            pl.BlockSpec((1, RET_DIM), lambda bi, hi, ti: (0, hi)),
            pl.BlockSpec((1, RET_DIM), lambda bi, hi, ti: (0, hi)),
        ],
        out_specs=[
            pl.BlockSpec((1, r, RET_DIM), lambda bi, hi, ti: (bi, ti, hi)),
            pl.BlockSpec((1, 1, RET_DIM, RET_DIM), lambda bi, hi, ti: (bi, hi, 0, 0)),
        ],
        out_shape=[
            jax.ShapeDtypeStruct((b, t, h * RET_DIM), F32),
            jax.ShapeDtypeStruct((b, h, RET_DIM, RET_DIM), F32),
        ],
        scratch_shapes=[pltpu.VMEM((RET_DIM, RET_DIM), F32)],
        compiler_params=_params("arbitrary", "arbitrary", "arbitrary"),
        name="retention",
    )(proj, proj, proj, proj, s0, cos2, sin2, dmask, q_dec, k_dec, c_dec,
      gn_g.reshape(1, -1), gn_b.reshape(1, -1))
    return o, s_new


def _shift_rows(x, shift, fill):
    rows = lax.broadcasted_iota(jnp.int32, x.shape, 0)
    return jnp.where(rows >= shift, pltpu.roll(x, shift, 0), fill)


def _lru_kernel(bx_ref, bg_ref, tail0_ref, h0_ref, cw_ref, cb_ref, wg_ref, ba_ref, bi_ref, lam_ref,
                ob_ref, hl_ref, tail_ref, tail_sc, h_sc, *, stride, pad):
    t = pl.program_id(1)
    r, w = bx_ref.shape[1], bx_ref.shape[2]

    @pl.when(t == 0)
    def _():
        tail_sc[...] = tail0_ref[0]
        h_sc[...] = h0_ref[0]

    bx = bx_ref[0]
    xp = jnp.concatenate([tail_sc[...], bx], axis=0)
    xc = cb_ref[...]
    for j in range(LRU_CONV):
        off = pad - (LRU_CONV - 1 - j) * stride
        xc = xc + xp[off:off + r] * cw_ref[j:j + 1, :]
    new_tail = xp[r:r + pad]
    tail_sc[...] = new_tail
    tail_ref[0] = new_tail

    z = _dot(xc.astype(BF16), wg_ref[...])
    rg = jax.nn.sigmoid(z[:, :w] + ba_ref[...])
    ig = jax.nn.sigmoid(z[:, w:] + bi_ref[...])
    nl = -lam_ref[...]
    softplus = jnp.maximum(nl, 0.0) + jnp.log1p(jnp.exp(-jnp.abs(nl)))
    log_a = -LRU_C * rg * softplus
    a = jnp.exp(log_a)
    th = jnp.tanh(log_a)
    u = jnp.sqrt(-2.0 * th / (1.0 - th)) * (ig * xc)

    d = 1
    while d < r // stride:
        u = u + a * _shift_rows(u, d * stride, 0.0)
        a = a * _shift_rows(a, d * stride, 1.0)
        d *= 2
    hc = h_sc[...]
    if r > stride:
        hc = jnp.tile(hc, (r // stride, 1)) if stride > 1 else jnp.broadcast_to(hc, (r, w))
    hs = u + a * hc
    h_last = hs[r - stride:, :]
    h_sc[...] = h_last
    hl_ref[0] = h_last
    ob_ref[0] = hs * _gelu(bg_ref[0])


def _lru(proj, tail0, h0, cw, cb, wg_bf16, ba, bi, lam, *, stride, rows_per_step):
    b, t, _ = proj.shape
    w = cw.shape[1]
    pad = tail0.shape[1]
    r = rows_per_step
    vec = lambda: pl.BlockSpec((1, w), lambda bi_, ti: (0, 0))
    ob, h_last, tail = pl.pallas_call(
        functools.partial(_lru_kernel, stride=stride, pad=pad),
        grid=(b, t // r),
        in_specs=[
            pl.BlockSpec((1, r, w), lambda bi_, ti: (bi_, ti, 4)),
            pl.BlockSpec((1, r, w), lambda bi_, ti: (bi_, ti, 5)),
            pl.BlockSpec((1, pad, w), lambda bi_, ti: (bi_, 0, 0)),
            pl.BlockSpec((1, stride, w), lambda bi_, ti: (bi_, 0, 0)),
            pl.BlockSpec((LRU_CONV, w), lambda bi_, ti: (0, 0)),
            vec(),
            pl.BlockSpec((w, 2 * w), lambda bi_, ti: (0, 0)),
            vec(), vec(), vec(),
        ],
        out_specs=[
            pl.BlockSpec((1, r, w), lambda bi_, ti: (bi_, ti, 0)),
            pl.BlockSpec((1, stride, w), lambda bi_, ti: (bi_, 0, 0)),
            pl.BlockSpec((1, pad, w), lambda bi_, ti: (bi_, 0, 0)),
        ],
        out_shape=[
            jax.ShapeDtypeStruct((b, t, w), F32),
            jax.ShapeDtypeStruct((b, stride, w), F32),
            jax.ShapeDtypeStruct((b, pad, w), F32),
        ],
        scratch_shapes=[pltpu.VMEM((pad, w), F32), pltpu.VMEM((stride, w), F32)],
        compiler_params=_params("arbitrary", "arbitrary"),
        name="rglru",
    )(proj, proj, tail0, h0, cw, cb.reshape(1, w), wg_bf16, ba.reshape(1, w), bi.reshape(1, w),
      lam.reshape(1, w))
    return ob, h_last, tail


def _block_diag(wb):
    nb, c, _ = wb.shape
    eye = jnp.eye(nb, dtype=wb.dtype)
    return (eye[:, None, :, None] * wb[:, :, None, :]).reshape(nb * c, nb * c)


def _post_kernel(*refs, n_mix, stride, pad):
    h_ref = refs[0]
    a_refs = refs[1:1 + n_mix]
    wo_refs = refs[1 + n_mix:1 + 2 * n_mix]
    (p_ref, nf_ref, wg_ref, wu_ref, wout_ref, cw_ref, cb_ref, cst_ref, pgn_ref, wpg_ref, wpp_ref, ppn_ref,
     o_ref, fc_ref, h1_sc, xn_sc, acc_sc, carry_sc) = refs[1 + 2 * n_mix:]
    i = pl.program_id(1)
    j = pl.program_id(2)
    tm = h_ref.shape[0]

    @pl.when(j == 0)
    def _():
        y = h_ref[...]
        for a_ref, wo_ref in zip(a_refs, wo_refs):
            y = y + _dot(a_ref[...].astype(BF16), wo_ref[...])
        h1_sc[...] = y
        xn_sc[...] = _rms(y, nf_ref[...]).astype(BF16)

    @pl.when(i == 0)
    def _():
        carry_sc[j] = cst_ref[0]

    xn = xn_sc[...]
    g = _dot(xn, wg_ref[...])
    u = _dot(xn, wu_ref[...])
    xp = jnp.concatenate([carry_sc[j], g], axis=0)
    gc = cb_ref[...]
    for kk in range(FFN_CONV):
        off = pad - (FFN_CONV - 1 - kk) * stride
        gc = gc + xp[off:off + tm] * cw_ref[kk:kk + 1, :]
    new_carry = xp[tm:tm + pad]
    carry_sc[j] = new_carry

    @pl.when(i == pl.num_programs(1) - 1)
    def _():
        fc_ref[0, j] = new_carry

    contrib = _dot((_gelu(gc) * u).astype(BF16), wout_ref[...])

    @pl.when(j == 0)
    def _():
        acc_sc[...] = contrib

    @pl.when(j > 0)
    def _():
        acc_sc[...] += contrib

    @pl.when(j == pl.num_programs(2) - 1)
    def _():
        h2 = h1_sc[...] + acc_sc[...]
        gate = jax.nn.sigmoid(_dot(_rms(h2, pgn_ref[...]).astype(BF16), wpg_ref[...]))
        e = _dot(p_ref[...].astype(BF16), wpp_ref[...])
        o_ref[...] = h2 + _rms(e, ppn_ref[...]) * gate


def _post_mixer(h, mix, wo_bf16, p, nf, wffn_in_bf16, wffn_out_bf16, cw, cb, cst, pgn, wpg_bf16, wpp_bf16,
                ppn, *, n_seq, stride, tm, tf):
    n, d = h.shape
    dff = wffn_out_bf16.shape[0]
    nj = dff // tf
    pad = cst.shape[1]
    ni = n // n_seq // tm
    n_mix = len(mix)
    row = lambda width: pl.BlockSpec((tm, width), lambda b, i, j: (b * ni + i, 0))
    full = lambda shape: pl.BlockSpec(shape, lambda b, i, j: (0, 0))
    in_specs = (
        [row(d)] + [row(a.shape[1]) for a in mix] + [full(w_.shape) for w_ in wo_bf16]
        + [
            row(p.shape[1]),
            full((1, d)),
            pl.BlockSpec((d, tf), lambda b, i, j: (0, j)),
            pl.BlockSpec((d, tf), lambda b, i, j: (0, nj + j)),
            pl.BlockSpec((tf, d), lambda b, i, j: (j, 0)),
            pl.BlockSpec((FFN_CONV, tf), lambda b, i, j: (0, j)),
            pl.BlockSpec((1, tf), lambda b, i, j: (0, j)),
            pl.BlockSpec((1, pad, tf), lambda b, i, j: (b, 0, j)),
            full((1, d)),
            full(wpg_bf16.shape),
            full(wpp_bf16.shape),
            full((1, d)),
        ])
    out, fc = pl.pallas_call(
        functools.partial(_post_kernel, n_mix=n_mix, stride=stride, pad=pad),
        grid=(n_seq, ni, nj),
        in_specs=in_specs,
        out_specs=[
            pl.BlockSpec((tm, d), lambda b, i, j: (b * ni + i, 0)),
            pl.BlockSpec((1, nj, pad, tf), lambda b, i, j: (b, 0, 0, 0)),
        ],
        out_shape=[
            jax.ShapeDtypeStruct((n, d), F32),
            jax.ShapeDtypeStruct((n_seq, nj, pad, tf), F32),
        ],
        scratch_shapes=[
            pltpu.VMEM((tm, d), F32),
            pltpu.VMEM((tm, d), BF16),
            pltpu.VMEM((tm, d), F32),
            pltpu.VMEM((nj, pad, tf), F32),
        ],
        compiler_params=_params("arbitrary", "arbitrary", "arbitrary"),
        name="post_mixer",
    )(h, *mix, *wo_bf16, p, nf.reshape(1, d), wffn_in_bf16, wffn_in_bf16, wffn_out_bf16, cw,
      cb.reshape(1, dff), cst, pgn.reshape(1, d), wpg_bf16, wpp_bf16, ppn.reshape(1, d))
    return out, jnp.transpose(fc, (0, 2, 1, 3)).reshape(n_seq, pad, dff)


def _t5_bucket(dist):
    n = jnp.maximum(dist, 0)
    max_exact = REL_BUCKETS // 2
    nf = jnp.maximum(n, max_exact).astype(F32)
    large = max_exact + (jnp.log(nf / max_exact) / math.log(REL_MAX_DIST / max_exact)
                         * (REL_BUCKETS - max_exact)).astype(jnp.int32)
    large = jnp.minimum(large, REL_BUCKETS - 1)
    return jnp.where(n < max_exact, n, large)


def _bias_by_distance(rel_bias, max_dist):
    return rel_bias.astype(F32)[_t5_bucket(jnp.arange(max_dist))].T


def _top3(sc, colf, n):
    picks = []
    for _ in range(MOBA_TOPK):
        m = jnp.max(sc, axis=-1, keepdims=True)
        first = jnp.min(jnp.where(sc == m, colf, float(n)), axis=-1, keepdims=True)
        picks.append((first, m))
        sc = jnp.where(colf == first, -jnp.inf, sc)
    return picks


def _moba_prompt_kernel(q_ref, k_ref, v_ref, bo_ref, bp_ref, bf_ref, o_ref, km_sc):
    i = pl.program_id(2)
    nb = km_sc.shape[0]
    blk = MOBA_BLOCK
    hd = C_HEAD_DIM

    @pl.when(i == 0)
    def _():
        def body(jb, c):
            kb = k_ref[0, pl.ds(pl.multiple_of(jb * blk, blk), blk), :]
            km_sc[pl.ds(jb, 1), :] = jnp.sum(kb, axis=0, keepdims=True) * (1.0 / blk)
            return c
        lax.fori_loop(0, nb, body, 0)

    q = q_ref[0]
    lo = lax.broadcasted_iota(jnp.int32, (blk, LANES), 1) < hd
    q_a = jnp.where(lo, q, 0.0)
    q_b = jnp.where(lo, 0.0, q)
    km = km_sc[...]
    colf = lax.broadcasted_iota(jnp.int32, (blk, nb), 1).astype(F32)
    i_f = i.astype(F32)

    def select(qh):
        sc = lax.dot_general(qh, km, (((1,), (1,)), ((), ())), precision=lax.Precision.HIGHEST,
                             preferred_element_type=F32)
        sc = jnp.where(colf < i_f, sc, -jnp.inf)
        sel = jnp.zeros((blk, nb), F32)
        for first, m in _top3(sc, colf, nb):
            sel = jnp.where((colf == first) & (m > -jnp.inf), 1.0, sel)
        return sel

    sel_a = select(q_a)
    sel_b = select(q_b)
    qa16 = (q_a * (hd ** -0.5)).astype(BF16)
    qb16 = (q_b * (hd ** -0.5)).astype(BF16)

    def head_update(q16, kb, bias, rowmask, m, l):
        s = _dot_nt(q16, kb) + bias
        if rowmask is not None:
            s = jnp.where(rowmask, s, NEG)
        m_new = jnp.maximum(m, jnp.max(s, axis=-1, keepdims=True))
        p = jnp.exp(s - m_new)
        alpha = jnp.exp(m - m_new)
        return p, alpha, m_new, alpha * l + jnp.sum(p, axis=-1, keepdims=True)

    def block_update(carry, jb, bias_a, bias_b, mask_a, mask_b):
        m_a, l_a, m_b, l_b, acc = carry
        start = pl.multiple_of(jb * blk, blk)
        kb = k_ref[0, pl.ds(start, blk), :].astype(BF16)
        vb = v_ref[0, pl.ds(start, blk), :].astype(BF16)
        p_a, al_a, m_a, l_a = head_update(qa16, kb, bias_a, mask_a, m_a, l_a)
        p_b, al_b, m_b, l_b = head_update(qb16, kb, bias_b, mask_b, m_b, l_b)
        pv = jnp.where(lo, _dot(p_a.astype(BF16), vb), _dot(p_b.astype(BF16), vb))
        acc = jnp.where(lo, al_a, al_b) * acc + pv
        return m_a, l_a, m_b, l_b, acc

    def sel_col(sel, jf):
        return jnp.sum(jnp.where(colf == jf, sel, 0.0), axis=-1, keepdims=True) > 0.0

    col1 = lambda v: jnp.full((blk, 1), v, F32)
    carry = (col1(NEG), col1(0.0), col1(NEG), col1(0.0), jnp.zeros((blk, LANES), F32))
    carry = block_update(carry, i, bo_ref[0], bo_ref[1], None, None)
    jp = jnp.maximum(i - 1, 0)
    jpf = i_f - 1.0
    carry = block_update(carry, jp, bp_ref[0], bp_ref[1], sel_col(sel_a, jpf), sel_col(sel_b, jpf))

    def far(jb, c):
        jf = jb.astype(F32)
        return block_update(c, jb, bf_ref[0], bf_ref[1], sel_col(sel_a, jf), sel_col(sel_b, jf))

    m_a, l_a, m_b, l_b, acc = lax.fori_loop(0, jnp.maximum(i - 1, 0), far, carry)
    o_ref[0] = acc / jnp.where(lo, l_a, l_b)


def _moba_prompt(q, k, v, rel_bias):
    b, t, width = q.shape
    blk = MOBA_BLOCK
    nb = t // blk
    hp = width // LANES
    tbl = _bias_by_distance(rel_bias, 2 * blk)
    ii = np.arange(blk)[:, None]
    ll = np.arange(blk)[None, :]
    bias_own = jnp.where(ii >= ll, tbl[:, np.maximum(ii - ll, 0)], NEG)
    bias_prev = tbl[:, blk + ii - ll]
    bias_far = tbl[:, 2 * blk - 1].reshape(-1, 1, 1)
    pair = lambda shape: pl.BlockSpec((2,) + shape, lambda bi, hi, qi: (hi, 0, 0))
    return pl.pallas_call(
        _moba_prompt_kernel,
        grid=(b, hp, nb),
        in_specs=[
            pl.BlockSpec((1, blk, LANES), lambda bi, hi, qi: (bi, qi, hi)),
            pl.BlockSpec((1, t, LANES), lambda bi, hi, qi: (bi, 0, hi)),
            pl.BlockSpec((1, t, LANES), lambda bi, hi, qi: (bi, 0, hi)),
            pair((blk, blk)), pair((blk, blk)), pair((1, 1)),
        ],
        out_specs=pl.BlockSpec((1, blk, LANES), lambda bi, hi, qi: (bi, qi, hi)),
        out_shape=jax.ShapeDtypeStruct((b, t, width), F32),
        scratch_shapes=[pltpu.VMEM((nb, LANES), F32)],
        compiler_params=_params("arbitrary", "arbitrary", "arbitrary"),
        name="moba_prompt",
    )(q, k, v, bias_own, bias_prev, bias_far)


def _kmean_kernel(pt_ref, k0_ref, k1_ref, o_ref):
    j = pl.program_id(1)

    @pl.when(j == 0)
    def _():
        o_ref[...] = jnp.zeros(o_ref.shape, F32)

    col = jnp.sum(k0_ref[0] + k1_ref[0], axis=-1, keepdims=True) * (1.0 / MOBA_BLOCK)
    lane = lax.broadcasted_iota(jnp.int32, o_ref.shape[1:], 2)
    o_ref[0] = jnp.where(lane == j, col, o_ref[0])


def _block_means(ck_t, page_table, nb):
    _, h, dh, page = ck_t.shape
    b = page_table.shape[0]
    return pl.pallas_call(
        _kmean_kernel,
        grid_spec=pltpu.PrefetchScalarGridSpec(
            num_scalar_prefetch=1,
            grid=(b, nb),
            in_specs=[
                pl.BlockSpec((1, h, dh, page), lambda bi, j, pt: (pt[bi, 2 * j], 0, 0, 0)),
                pl.BlockSpec((1, h, dh, page), lambda bi, j, pt: (pt[bi, 2 * j + 1], 0, 0, 0)),
            ],
            out_specs=pl.BlockSpec((1, h, dh, nb), lambda bi, j, pt: (bi, 0, 0, 0)),
        ),
        out_shape=jax.ShapeDtypeStruct((b, h, dh, nb), F32),
        compiler_params=_params("arbitrary", "arbitrary"),
        name="moba_block_means",
    )(page_table, ck_t, ck_t)


def _select_kernel(qt_ref, km_ref, o_ref):
    s_len, _, h = qt_ref.shape[1:]
    nb = km_ref.shape[3]
    colf = lax.broadcasted_iota(jnp.int32, (1, nb), 1).astype(F32)
    lane = lax.broadcasted_iota(jnp.int32, (1, LANES), 1)
    for s in range(s_len):
        qt = qt_ref[0, s]
        for hh in range(h):
            sc = jnp.sum(km_ref[0, hh] * qt[:, hh:hh + 1], axis=0, keepdims=True)
            picks = [jnp.minimum(first, nb - 1.0).astype(jnp.int32) for first, _ in _top3(sc, colf, nb)]
            row = jnp.where(lane == 0, picks[0], jnp.where(lane == 1, picks[1], picks[2]))
            o_ref[0, pl.ds(s * h + hh, 1), :] = row


def _select_blocks(q_t, kmean_t):
    b, s_len, dh, h = q_t.shape
    nb = kmean_t.shape[3]
    out = pl.pallas_call(
        _select_kernel,
        grid=(b,),
        in_specs=[
            pl.BlockSpec((1, s_len, dh, h), lambda bi: (bi, 0, 0, 0)),
            pl.BlockSpec((1, h, dh, nb), lambda bi: (bi, 0, 0, 0)),
        ],
        out_specs=pl.BlockSpec((1, s_len * h, LANES), lambda bi: (bi, 0, 0)),
        out_shape=jax.ShapeDtypeStruct((b, s_len * h, LANES), jnp.int32),
        compiler_params=_params("arbitrary"),
        name="moba_select",
    )(q_t, kmean_t)
    return out[:, :, :MOBA_TOPK].reshape(b, s_len, h, MOBA_TOPK)


def _moba_sample_kernel(pt_ref, sel_ref, ck_ref, cv_ref, qt_ref, kn_ref, vn_ref, bl_ref, bo_ref, bf_ref,
                        o_ref, kbuf, vbuf, sem, *, nb):
    b = pl.program_id(0)
    s = pl.program_id(1)
    s_len = pl.num_programs(1)
    h, _, dh, _ = kbuf.shape
    ppb = 2

    def copies(hh, n, pg):
        idx = sel_ref[((b * s_len + s) * h + hh) * MOBA_TOPK + n]
        phys = pt_ref[b, idx * ppb + pg]
        slot = n * ppb + pg
        return (pltpu.make_async_copy(ck_ref.at[phys, hh], kbuf.at[hh, slot], sem.at[0, hh, slot]),
                pltpu.make_async_copy(cv_ref.at[phys, hh], vbuf.at[hh, slot], sem.at[1, hh, slot]))

    slots = [(hh, n, pg) for hh in range(h) for n in range(MOBA_TOPK) for pg in range(ppb)]
    for sl in slots:
        for cp in copies(*sl):
            cp.start()
    for sl in slots:
        for cp in copies(*sl):
            cp.wait()

    lane = lax.broadcasted_iota(jnp.int32, (dh, LANES), 1)
    qt = qt_ref[0, 0] * (dh ** -0.5)
    out = jnp.zeros((dh, LANES), F32)
    for hh in range(h):
        qc = qt[:, hh:hh + 1]
        far = bf_ref[hh:hh + 1, :]
        logits = []
        for n in range(MOBA_TOPK):
            idx = sel_ref[((b * s_len + s) * h + hh) * MOBA_TOPK + n]
            for pg in range(ppb):
                lg = jnp.sum(kbuf[hh, n * ppb + pg] * qc, axis=0, keepdims=True)
                logits.append(lg + jnp.where(idx == nb - 1, bl_ref[0, hh, pg:pg + 1, :], far))
        lown = jnp.sum(kn_ref[0, hh] * qc, axis=0, keepdims=True) + bo_ref[0, hh:hh + 1, :]
        m = jnp.max(lown, axis=-1, keepdims=True)
        for lg in logits:
            m = jnp.maximum(m, jnp.max(lg, axis=-1, keepdims=True))
        pown = jnp.exp(lown - m)
        den = jnp.sum(pown, axis=-1, keepdims=True)
        col = jnp.sum(vn_ref[0, hh] * pown, axis=-1, keepdims=True)
        for slot, lg in enumerate(logits):
            p = jnp.exp(lg - m)
            den = den + jnp.sum(p, axis=-1, keepdims=True)
            col = col + jnp.sum(vbuf[hh, slot] * p, axis=-1, keepdims=True)
        out = jnp.where(lane == hh, col / den, out)
    o_ref[0, 0] = out


def _moba_sample(q, k_new, v_new, ck_t, cv_t, page_table, rel_bias, past_len):
    b, s_len, h, dh = q.shape
    page = ck_t.shape[3]
    blk = MOBA_BLOCK
    assert past_len % blk == 0 and blk == 2 * page
    nb = past_len // blk
    assert nb >= MOBA_TOPK
    q_t = jnp.transpose(q, (0, 1, 3, 2))
    kmean_t = _block_means(ck_t, page_table, nb)
    sel = _select_blocks(q_t, kmean_t)

    tbl = _bias_by_distance(rel_bias, 2 * blk)
    ss = np.arange(s_len)
    bias_last = tbl[:, blk + ss[:, None] - np.arange(blk)[None, :]]
    bias_last = jnp.transpose(bias_last, (1, 0, 2)).reshape(s_len, h, 2, page)
    own_d = ss[:, None] - ss[None, :]
    bias_own = jnp.where(own_d >= 0, tbl[:, np.maximum(own_d, 0)], NEG)
    bias_own = jnp.transpose(bias_own, (1, 0, 2))
    bias_far = tbl[:, 2 * blk - 1].reshape(h, 1)
    kn_t = jnp.transpose(k_new, (0, 2, 3, 1))
    vn_t = jnp.transpose(v_new, (0, 2, 3, 1))

    out = pl.pallas_call(
        functools.partial(_moba_sample_kernel, nb=nb),
        grid_spec=pltpu.PrefetchScalarGridSpec(
            num_scalar_prefetch=2,
            grid=(b, s_len),
            in_specs=[
                pl.BlockSpec(memory_space=pl.ANY),
                pl.BlockSpec(memory_space=pl.ANY),
                pl.BlockSpec((1, 1, dh, h), lambda bi, si, pt, sl: (bi, si, 0, 0)),
                pl.BlockSpec((1, h, dh, s_len), lambda bi, si, pt, sl: (bi, 0, 0, 0)),
                pl.BlockSpec((1, h, dh, s_len), lambda bi, si, pt, sl: (bi, 0, 0, 0)),
                pl.BlockSpec((1, h, 2, page), lambda bi, si, pt, sl: (si, 0, 0, 0)),
                pl.BlockSpec((1, h, s_len), lambda bi, si, pt, sl: (si, 0, 0)),
                pl.BlockSpec((h, 1), lambda bi, si, pt, sl: (0, 0)),
            ],
            out_specs=pl.BlockSpec((1, 1, dh, LANES), lambda bi, si, pt, sl: (bi, si, 0, 0)),
            scratch_shapes=[
                pltpu.VMEM((h, 2 * MOBA_TOPK, dh, page), F32),
                pltpu.VMEM((h, 2 * MOBA_TOPK, dh, page), F32),
                pltpu.SemaphoreType.DMA((2, h, 2 * MOBA_TOPK)),
            ],
        ),
        out_shape=jax.ShapeDtypeStruct((b, s_len, dh, LANES), F32),
        compiler_params=_params("arbitrary", "arbitrary"),
        name="moba_sample",
    )(page_table, sel.reshape(-1), ck_t, cv_t, q_t, kn_t, vn_t, bias_last, bias_own, bias_far)
    return jnp.transpose(out[:, :, :, :h], (0, 1, 3, 2))


def _prep_weights(w):
    bf = lambda a: a.astype(BF16)
    n_even = w['w0_in'].shape[0]
    n_odd = w['c_in'].shape[0]
    out = dict(w)
    out['w0_in'] = bf(w['w0_in'])
    out['w0_out'] = bf(w['w0_out'])
    out['lru_wg'] = [bf(jnp.concatenate([_block_diag(w['lru_wa'][e]), _block_diag(w['lru_wi'][e])], axis=1))
                     for e in range(n_even)]
    out['c_in'] = bf(w['c_in'])
    out['c_out'] = bf(w['c_out'])
    out['c_gain'] = [jnp.concatenate([jnp.tile(w['c_qn'][o], C_HEADS), jnp.tile(w['c_kn'][o], C_HEADS),
                                      jnp.ones((C_HEADS * C_HEAD_DIM,), F32)]).reshape(1, -1)
                     for o in range(n_odd)]
    out['ffn_in'] = bf(w['ffn_in'])
    out['ffn_out'] = bf(w['ffn_out'])
    out['ple_proj'] = bf(w['ple_proj'])
    out['ple_gate'] = bf(w['ple_gate'])
    return out


def _post(w, l, h, mix, wo, p, cst, *, n_seq, stride, tm, tf):
    return _post_mixer(h, mix, wo, p, w['norm_ffn'][l], w['ffn_in'][l], w['ffn_out'][l], w['ffn_conv_w'][l],
                       w['ffn_conv_b'][l], cst, w['ple_gate_norm'][l], w['ple_gate'][l], w['ple_proj'][l],
                       w['ple_post_norm'][l], n_seq=n_seq, stride=stride, tm=tm, tf=tf)


def _pad_rows_front(a, pad):
    return jnp.pad(a, ((0, 0), (pad - a.shape[1], 0), (0, 0)))


def _prompt_trunk(x, p, w, rel_bias):
    b, t, d = x.shape
    n = b * t
    depth = p.shape[0]
    h = x.reshape(n, d)
    tm = 512
    rets, lrus, lconvs, fconvs, ks, vs = [], [], [], [], [], []
    lw = w['lru_conv_w'].shape[2]
    dff = w['ffn_out'].shape[1]
    for l in range(depth):
        cst = jnp.zeros((b, SUBLANES, dff), F32)
        if l % 2 == 0:
            e = l // 2
            proj = _norm_proj(h, w['norm_mix'][l], w['w0_in'][e], tm=512, tn=1024).reshape(b, t, -1)
            o, s_new = _retention(proj, jnp.zeros((b, RET_HEADS, RET_DIM, RET_DIM), F32), jnp.arange(t),
                                  w['ret_gn_g'][e], w['ret_gn_b'][e], c_eff=RET_CHUNK, c_pad=RET_CHUNK,
                                  rows_per_step=512)
            ob, h_last, tail = _lru(proj, jnp.zeros((b, SUBLANES, lw), F32), jnp.zeros((b, 1, lw), F32),
                                    w['lru_conv_w'][e], w['lru_conv_b'][e], w['lru_wg'][e], w['lru_ba'][e],
                                    w['lru_bi'][e], w['lru_lam'][e], stride=1, rows_per_step=512)
            rets.append(s_new)
            lrus.append(h_last[:, 0])
            lconvs.append(tail[:, SUBLANES - (LRU_CONV - 1):])
            rw = o.shape[-1]
            mix = [o.reshape(n, rw), ob.reshape(n, lw)]
            wo = [w['w0_out'][e][:rw], w['w0_out'][e][rw:]]
        else:
            o_ = l // 2
            q, k, v = _norm_proj(h, w['norm_mix'][l], w['c_in'][o_], tm=512, tn=1024, head_gain=w['c_gain'][o_],
                                 n_rms_tiles=2, split=True)
            cw = q.shape[1]
            att = _moba_prompt(q.reshape(b, t, cw), k.reshape(b, t, cw), v.reshape(b, t, cw), rel_bias)
            ks.append(k.reshape(b, t, C_HEADS, C_HEAD_DIM))
            vs.append(v.reshape(b, t, C_HEADS, C_HEAD_DIM))
            mix = [att.reshape(n, cw)]
            wo = [w['c_out'][o_]]
        h, fc = _post(w, l, h, mix, wo, p[l].reshape(n, -1), cst, n_seq=b, stride=1, tm=tm, tf=512)
        fconvs.append(fc[:, SUBLANES - (FFN_CONV - 1):])
    return (h.reshape(b, t, d), jnp.stack(rets), jnp.stack(lrus), jnp.stack(lconvs), jnp.stack(fconvs),
            jnp.stack(ks), jnp.stack(vs))


def _sample_trunk(x, p, ret0, lru0, lconv0, fconv0, cache_k, cache_v, page_table, w, rel_bias):
    b, t, d = x.shape
    n = b * t
    depth = p.shape[0]
    past_len = page_table.shape[1] * cache_k.shape[2]
    tmaj = lambda a: jnp.transpose(a, (1, 0) + tuple(range(2, a.ndim)))
    h = tmaj(x).reshape(n, d)
    rets, lrus, lconvs, fconvs, ks, vs = [], [], [], [], [], []
    lw = w['lru_conv_w'].shape[2]
    for l in range(depth):
        cst = tmaj(fconv0[l]).reshape(1, (FFN_CONV - 1) * b, -1)
        if l % 2 == 0:
            e = l // 2
            proj = _norm_proj(h, w['norm_mix'][l], w['w0_in'][e], tm=n, tn=1024)
            rcols = 4 * RET_HEADS * RET_DIM
            pr = tmaj(proj[:, :rcols].reshape(t, b, rcols))
            pr = jnp.pad(pr, ((0, 0), (0, SUBLANES - t), (0, 0)))
            pos = past_len + jnp.arange(SUBLANES)
            o, s_new = _retention(pr, ret0[e].astype(F32), pos, w['ret_gn_g'][e], w['ret_gn_b'][e],
                                  c_eff=t, c_pad=SUBLANES, rows_per_step=SUBLANES)
            o = tmaj(o[:, :t]).reshape(n, -1)
            tail0 = tmaj(lconv0[e]).reshape(1, (LRU_CONV - 1) * b, lw)
            ob, h_last, tail = _lru(proj.reshape(1, n, -1), tail0, lru0[e].reshape(1, b, lw),
                                    w['lru_conv_w'][e], w['lru_conv_b'][e], w['lru_wg'][e], w['lru_ba'][e],
                                    w['lru_bi'][e], w['lru_lam'][e], stride=b, rows_per_step=n)
            rets.append(s_new)
            lrus.append(h_last[0])
            lconvs.append(tmaj(tail.reshape(LRU_CONV - 1, b, lw)))
            rw = o.shape[-1]
            mix = [o, ob.reshape(n, lw)]
            wo = [w['w0_out'][e][:rw], w['w0_out'][e][rw:]]
        else:
            o_ = l // 2
            q, k, v = _norm_proj(h, w['norm_mix'][l], w['c_in'][o_], tm=n, tn=1024, head_gain=w['c_gain'][o_],
                                 n_rms_tiles=2, split=True)
            heads = lambda a: tmaj(a.reshape(t, b, C_HEADS, C_HEAD_DIM))
            q, k, v = heads(q), heads(k), heads(v)
            ck_t = jnp.transpose(cache_k[o_], (0, 2, 3, 1))
            cv_t = jnp.transpose(cache_v[o_], (0, 2, 3, 1))
            att = _moba_sample(q, k, v, ck_t, cv_t, page_table, rel_bias, past_len)
            ks.append(k)
            vs.append(v)
            mix = [tmaj(att).reshape(n, -1)]
            wo = [w['c_out'][o_]]
        h, fc = _post(w, l, h, mix, wo, tmaj(p[l]).reshape(n, -1), cst, n_seq=1, stride=b, tm=n, tf=512)
        fconvs.append(tmaj(fc.reshape(FFN_CONV - 1, b, -1)))
    return (tmaj(h.reshape(t, b, d)), jnp.stack(rets), jnp.stack(lrus), jnp.stack(lconvs), jnp.stack(fconvs),
            jnp.stack(ks), jnp.stack(vs))


def kernel(x_prompt, x_sample, state_ret, state_lru, state_lru_conv, state_ffn_conv, cache_k, cache_v, page_table, p_prompt, p_sample, norm_mix, norm_ffn, w0_in, ret_gn_g, ret_gn_b, lru_conv_w, lru_conv_b, lru_wa, lru_ba, lru_wi, lru_bi, lru_lam, w0_out, c_in, c_qn, c_kn, c_out, rel_bias, ffn_in, ffn_conv_w, ffn_conv_b, ffn_out, ple_proj, ple_gate_norm, ple_gate, ple_post_norm):
    w = _prep_weights(dict(
        norm_mix=norm_mix, norm_ffn=norm_ffn, w0_in=w0_in, ret_gn_g=ret_gn_g, ret_gn_b=ret_gn_b,
        lru_conv_w=lru_conv_w, lru_conv_b=lru_conv_b, lru_wa=lru_wa, lru_ba=lru_ba, lru_wi=lru_wi,
        lru_bi=lru_bi, lru_lam=lru_lam, w0_out=w0_out, c_in=c_in, c_qn=c_qn, c_kn=c_kn, c_out=c_out,
        ffn_in=ffn_in, ffn_conv_w=ffn_conv_w, ffn_conv_b=ffn_conv_b, ffn_out=ffn_out, ple_proj=ple_proj,
        ple_gate_norm=ple_gate_norm, ple_gate=ple_gate, ple_post_norm=ple_post_norm))
    y_p, ret_p, lru_p, lconv_p, fconv_p, k_p, v_p = _prompt_trunk(x_prompt, p_prompt, w, rel_bias)
    y_s, ret_s, lru_s, lconv_s, fconv_s, k_s, v_s = _sample_trunk(
        x_sample, p_sample, state_ret, state_lru, state_lru_conv, state_ffn_conv, cache_k, cache_v, page_table,
        w, rel_bias)
    return (y_p, y_s, ret_p, ret_s, lru_p, lru_s, lconv_p, lconv_s, fconv_p, fconv_s, k_p, k_s, v_p, v_s)
```

```python
import functools
import math

import numpy as np
import jax
import jax.numpy as jnp
from jax import lax
from jax.experimental import pallas as pl
from jax.experimental.pallas import tpu as pltpu

F32 = jnp.float32
BF16 = jnp.bfloat16
EPS = 1e-6
NEG = -1e30

LANES = 128
SUBLANES = 8
VMEM_LIMIT_BYTES = 48 * 1024 * 1024

RET_HEADS = 4
RET_DIM = 128
RET_CHUNK = 128
ROPE_THETA = 10000.0
LRU_C = 8.0
LRU_CONV = 4
FFN_CONV = 3
C_HEADS = 16
C_HEAD_DIM = 64
MOBA_BLOCK = 256
MOBA_TOPK = 3
REL_BUCKETS = 32
REL_MAX_DIST = 128
LOG2E = 1.4426950408889634
MASK_ROWS = 32
PV_ONES = 16
GROUP = 2
KMEAN_BLOCKS_PER_STEP = 4
PAGES_PER_BLOCK = 2


def _params(*sem):
    return pltpu.CompilerParams(dimension_semantics=sem, vmem_limit_bytes=VMEM_LIMIT_BYTES)


def _rms(x, g):
    return x * lax.rsqrt(jnp.mean(x * x, axis=-1, keepdims=True) + EPS) * g


def _gelu(x):
    return 0.5 * x * (1.0 + jnp.tanh(0.7978845608028654 * (x + 0.044715 * (x * x * x))))


def _dot(a, b):
    return jnp.dot(a, b, preferred_element_type=F32)


def _dot_nt(a, b):
    return lax.dot_general(a, b, (((1,), (1,)), ((), ())), preferred_element_type=F32)


def _dot_tn(a, b):
    return lax.dot_general(a, b, (((0,), (0,)), ((), ())), preferred_element_type=F32)


def _head_rms(y, hd):
    tm, tn = y.shape
    lo = lax.broadcasted_iota(jnp.int32, (tm, LANES), 1) < hd
    outs = []
    for s in range(tn // LANES):
        ys = y[:, s * LANES:(s + 1) * LANES]
        y2 = ys * ys
        s0 = jnp.sum(jnp.where(lo, y2, 0.0), axis=-1, keepdims=True)
        s1 = jnp.sum(jnp.where(lo, 0.0, y2), axis=-1, keepdims=True)
        r = lax.rsqrt(jnp.where(lo, s0, s1) * (1.0 / hd) + EPS)
        outs.append(ys * r)
    return jnp.concatenate(outs, axis=-1)


def _norm_proj_kernel(x_ref, g_ref, w_ref, hg_ref, *rest, n_rms_tiles, plan):
    n_out = sum(len(p) for p in plan) if plan else 1
    o_refs, xn_ref = rest[:n_out], rest[n_out]
    j = pl.program_id(1)

    @pl.when(j == 0)
    def _():
        xn_ref[...] = _rms(x_ref[...], g_ref[...]).astype(BF16)

    y = _dot(xn_ref[...], w_ref[...])
    if not plan:
        o_refs[0][...] = y
        return
    k = 0
    for t, outs in enumerate(plan):
        @pl.when(j == t)
        def _(t=t, outs=outs, k=k):
            yt = _head_rms(y, C_HEAD_DIM) * hg_ref[...] if t < n_rms_tiles else y
            for off, transposed in enumerate(outs):
                if transposed:
                    o_refs[k + off][0] = yt.T
                else:
                    o_refs[k + off][...] = yt
        k += len(outs)


def _norm_proj(x, g, w_bf16, *, tm, tn, head_gain=None, n_rms_tiles=0, plan=None, n_seq=1):
    n, d = x.shape
    nout = w_bf16.shape[1]
    nj = nout // tn
    if head_gain is None:
        head_gain = jnp.ones((1, nout), F32)
    if plan is None:
        out_shape = [jax.ShapeDtypeStruct((n, nout), F32)]
        out_specs = [pl.BlockSpec((tm, tn), lambda i, j: (i, j))]
    else:
        t_len = n // n_seq
        nt = t_len // tm
        out_shape, out_specs = [], []
        for outs in plan:
            for transposed in outs:
                if transposed:
                    out_shape.append(jax.ShapeDtypeStruct((n_seq, tn, t_len), F32))
                    out_specs.append(pl.BlockSpec((1, tn, tm), lambda i, j: (i // nt, 0, i % nt)))
                else:
                    out_shape.append(jax.ShapeDtypeStruct((n, tn), F32))
                    out_specs.append(pl.BlockSpec((tm, tn), lambda i, j: (i, 0)))
    outs = pl.pallas_call(
        functools.partial(_norm_proj_kernel, n_rms_tiles=n_rms_tiles, plan=plan),
        grid=(n // tm, nj),
        in_specs=[
            pl.BlockSpec((tm, d), lambda i, j: (i, 0)),
            pl.BlockSpec((1, d), lambda i, j: (0, 0)),
            pl.BlockSpec((d, tn), lambda i, j: (0, j)),
            pl.BlockSpec((1, tn), lambda i, j: (0, j)),
        ],
        out_specs=out_specs,
        out_shape=out_shape,
        scratch_shapes=[pltpu.VMEM((tm, d), BF16)],
        compiler_params=_params("arbitrary", "arbitrary"),
        name="norm_proj",
    )(x, g.reshape(1, d), w_bf16, head_gain)
    return outs[0] if plan is None else outs


def _ret_kernel(q_ref, k_ref, v_ref, g_ref, s0_ref, cos_ref, sin_ref, dm_ref, qd_ref, kd_ref, cd_ref,
                gg_ref, gb_ref, o_ref, s_ref, st_ref, *, c, nsub):
    t = pl.program_id(2)

    @pl.when(t == 0)
    def _():
        st_ref[...] = s0_ref[0, 0]

    dm = dm_ref[0]
    qd = qd_ref[0]
    kd = kd_ref[0]
    cd = cd_ref[0]
    gg = gg_ref[...]
    gb = gb_ref[...]
    for u in range(nsub):
        rows = slice(u * c, (u + 1) * c)
        cos = cos_ref[rows, :]
        sin = sin_ref[rows, :]
        q = q_ref[0, rows, :]
        k = k_ref[0, rows, :]
        q = q * cos + pltpu.roll(q, RET_DIM // 2, 1) * sin
        k = (k * cos + pltpu.roll(k, RET_DIM // 2, 1) * sin) * (RET_DIM ** -0.5)
        s = st_ref[...]
        qb = q.astype(BF16)
        vb = v_ref[0, rows, :].astype(BF16)
        att = _dot_nt(qb, k.astype(BF16)) * dm
        o = _dot(att.astype(BF16), vb) + _dot(qb, s.astype(BF16)) * qd
        st_ref[...] = s * cd + _dot_tn((k * kd).astype(BF16), vb)
        mu = jnp.mean(o, axis=-1, keepdims=True)
        oc = o - mu
        var = jnp.mean(oc * oc, axis=-1, keepdims=True)
        o = oc * lax.rsqrt(var + EPS) * gg + gb
        gate = g_ref[0, rows, :]
        o_ref[0, rows, :] = o * (gate * jax.nn.sigmoid(gate))

    @pl.when(t == pl.num_programs(2) - 1)
    def _():
        s_ref[0, 0] = st_ref[...]


def _retention(proj, s0, pos, gn_g, gn_b, *, c_eff, c_pad, rows_per_step):
    b, t, _ = proj.shape
    h = RET_HEADS
    half = RET_DIM // 2
    inv = ROPE_THETA ** (-jnp.arange(half, dtype=F32) / half)
    ang = pos.astype(F32)[:, None] * inv[None, :]
    cos2 = jnp.concatenate([jnp.cos(ang), jnp.cos(ang)], axis=-1)
    sin2 = jnp.concatenate([-jnp.sin(ang), jnp.sin(ang)], axis=-1)
    log_g = jnp.log1p(-jnp.exp2(-5.0 - jnp.arange(h, dtype=F32)))
    idx = jnp.arange(c_pad, dtype=F32)
    diff = idx[:, None] - idx[None, :]
    dmask = jnp.where(diff >= 0, jnp.exp(log_g[:, None, None] * jnp.maximum(diff, 0.0)), 0.0)
    q_dec = jnp.exp(log_g[:, None, None] * (idx[None, :, None] + 1.0))
    k_dec = jnp.exp(log_g[:, None, None] * (c_eff - 1.0 - idx[None, :, None]))
    c_dec = jnp.exp(log_g * c_eff).reshape(h, 1, 1)
    r = rows_per_step
    nsub = r // c_pad
    col = lambda off: pl.BlockSpec((1, r, RET_DIM), lambda bi, hi, ti, off=off: (bi, ti, off + hi))
    per_head = lambda shape: pl.BlockSpec((1,) + shape, lambda bi, hi, ti: (hi, 0, 0))
    o, s_new = pl.pallas_call(
        functools.partial(_ret_kernel, c=c_pad, nsub=nsub),
        grid=(b, h, t // r),
        in_specs=[
            col(0), col(h), col(2 * h), col(3 * h),
            pl.BlockSpec((1, 1, RET_DIM, RET_DIM), lambda bi, hi, ti: (bi, hi, 0, 0)),
            pl.BlockSpec((r, RET_DIM), lambda bi, hi, ti: (ti, 0)),
            pl.BlockSpec((r, RET_DIM), lambda bi, hi, ti: (ti, 0)),
            per_head((c_pad, c_pad)), per_head((c_pad, 1)), per_head((c_pad, 1)), per_head((1, 1)),
            pl.BlockSpec((1, RET_DIM), lambda bi, hi, ti: (0, hi)),
            pl.BlockSpec((1, RET_DIM), lambda bi, hi, ti: (0, hi)),
        ],
        out_specs=[
            pl.BlockSpec((1, r, RET_DIM), lambda bi, hi, ti: (bi, ti, hi)),
            pl.BlockSpec((1, 1, RET_DIM, RET_DIM), lambda bi, hi, ti: (bi, hi, 0, 0)),
        ],
        out_shape=[
            jax.ShapeDtypeStruct((b, t, h * RET_DIM), F32),
            jax.ShapeDtypeStruct((b, h, RET_DIM, RET_DIM), F32),
        ],
        scratch_shapes=[pltpu.VMEM((RET_DIM, RET_DIM), F32)],
        compiler_params=_params("arbitrary", "arbitrary", "arbitrary"),
        name="retention",
    )(proj, proj, proj, proj, s0, cos2, sin2, dmask, q_dec, k_dec, c_dec,
      gn_g.reshape(1, -1), gn_b.reshape(1, -1))
    return o, s_new


def _shift_rows(x, shift, fill):
    rows = lax.broadcasted_iota(jnp.int32, x.shape, 0)
    return jnp.where(rows >= shift, pltpu.roll(x, shift, 0), fill)


def _lru_kernel(bx_ref, bg_ref, tail0_ref, h0_ref, cw_ref, cb_ref, wg_ref, ba_ref, bi_ref, lam_ref,
                ob_ref, hl_ref, tail_ref, tail_sc, h_sc, *, stride, pad):
    t = pl.program_id(1)
    r, w = bx_ref.shape[1], bx_ref.shape[2]

    @pl.when(t == 0)
    def _():
        tail_sc[...] = tail0_ref[0]
        h_sc[...] = h0_ref[0]

    bx = bx_ref[0]
    xp = jnp.concatenate([tail_sc[...], bx], axis=0)
    xc = cb_ref[...]
    for j in range(LRU_CONV):
        off = pad - (LRU_CONV - 1 - j) * stride
        xc = xc + xp[off:off + r] * cw_ref[j:j + 1, :]
    new_tail = xp[r:r + pad]
    tail_sc[...] = new_tail
    tail_ref[0] = new_tail

    z = _dot(xc.astype(BF16), wg_ref[...])
    rg = jax.nn.sigmoid(z[:, :w] + ba_ref[...])
    ig = jax.nn.sigmoid(z[:, w:] + bi_ref[...])
    nl = -lam_ref[...]
    softplus = jnp.maximum(nl, 0.0) + jnp.log1p(jnp.exp(-jnp.abs(nl)))
    log_a = -LRU_C * rg * softplus
    a = jnp.exp(log_a)
    th = jnp.tanh(log_a)
    u = jnp.sqrt(-2.0 * th / (1.0 - th)) * (ig * xc)

    d = 1
    while d < r // stride:
        u = u + a * _shift_rows(u, d * stride, 0.0)
        a = a * _shift_rows(a, d * stride, 1.0)
        d *= 2
    hc = h_sc[...]
    if r > stride:
        hc = jnp.tile(hc, (r // stride, 1)) if stride > 1 else jnp.broadcast_to(hc, (r, w))
    hs = u + a * hc
    h_last = hs[r - stride:, :]
    h_sc[...] = h_last
    hl_ref[0] = h_last
    ob_ref[0] = hs * _gelu(bg_ref[0])


def _lru(proj, tail0, h0, cw, cb, wg_bf16, ba, bi, lam, *, stride, rows_per_step):
    b, t, _ = proj.shape
    w = cw.shape[1]
    pad = tail0.shape[1]
    r = rows_per_step
    vec = lambda: pl.BlockSpec((1, w), lambda bi_, ti: (0, 0))
    ob, h_last, tail = pl.pallas_call(
        functools.partial(_lru_kernel, stride=stride, pad=pad),
        grid=(b, t // r),
        in_specs=[
            pl.BlockSpec((1, r, w), lambda bi_, ti: (bi_, ti, 4)),
            pl.BlockSpec((1, r, w), lambda bi_, ti: (bi_, ti, 5)),
            pl.BlockSpec((1, pad, w), lambda bi_, ti: (bi_, 0, 0)),
            pl.BlockSpec((1, stride, w), lambda bi_, ti: (bi_, 0, 0)),
            pl.BlockSpec((LRU_CONV, w), lambda bi_, ti: (0, 0)),
            vec(),
            pl.BlockSpec((w, 2 * w), lambda bi_, ti: (0, 0)),
            vec(), vec(), vec(),
        ],
        out_specs=[
            pl.BlockSpec((1, r, w), lambda bi_, ti: (bi_, ti, 0)),
            pl.BlockSpec((1, stride, w), lambda bi_, ti: (bi_, 0, 0)),
            pl.BlockSpec((1, pad, w), lambda bi_, ti: (bi_, 0, 0)),
        ],
        out_shape=[
            jax.ShapeDtypeStruct((b, t, w), F32),
            jax.ShapeDtypeStruct((b, stride, w), F32),
            jax.ShapeDtypeStruct((b, pad, w), F32),
        ],
        scratch_shapes=[pltpu.VMEM((pad, w), F32), pltpu.VMEM((stride, w), F32)],
        compiler_params=_params("arbitrary", "arbitrary"),
        name="rglru",
    )(proj, proj, tail0, h0, cw, cb.reshape(1, w), wg_bf16, ba.reshape(1, w), bi.reshape(1, w),
      lam.reshape(1, w))
    return ob, h_last, tail


def _block_diag(wb):
    nb, c, _ = wb.shape
    eye = jnp.eye(nb, dtype=wb.dtype)
    return (eye[:, None, :, None] * wb[:, :, None, :]).reshape(nb * c, nb * c)


def _post_kernel(*refs, n_mix, stride, pad):
    h_ref = refs[0]
    a_refs = refs[1:1 + n_mix]
    wo_refs = refs[1 + n_mix:1 + 2 * n_mix]
    (p_ref, nf_ref, wg_ref, wu_ref, wout_ref, cw_ref, cb_ref, cst_ref, pgn_ref, wpg_ref, wpp_ref, ppn_ref,
     o_ref, fc_ref, h1_sc, xn_sc, acc_sc, carry_sc) = refs[1 + 2 * n_mix:]
    i = pl.program_id(1)
    j = pl.program_id(2)
    tm = h_ref.shape[0]

    @pl.when(j == 0)
    def _():
        y = h_ref[...]
        for a_ref, wo_ref in zip(a_refs, wo_refs):
            y = y + _dot(a_ref[...].astype(BF16), wo_ref[...])
        h1_sc[...] = y
        xn_sc[...] = _rms(y, nf_ref[...]).astype(BF16)

    @pl.when(i == 0)
    def _():
        carry_sc[j] = cst_ref[0]

    xn = xn_sc[...]
    g = _dot(xn, wg_ref[...])
    u = _dot(xn, wu_ref[...])
    xp = jnp.concatenate([carry_sc[j], g], axis=0)
    gc = cb_ref[...]
    for kk in range(FFN_CONV):
        off = pad - (FFN_CONV - 1 - kk) * stride
        gc = gc + xp[off:off + tm] * cw_ref[kk:kk + 1, :]
    new_carry = xp[tm:tm + pad]
    carry_sc[j] = new_carry

    @pl.when(i == pl.num_programs(1) - 1)
    def _():
        fc_ref[0, j] = new_carry

    contrib = _dot((_gelu(gc) * u).astype(BF16), wout_ref[...])

    @pl.when(j == 0)
    def _():
        acc_sc[...] = contrib

    @pl.when(j > 0)
    def _():
        acc_sc[...] += contrib

    @pl.when(j == pl.num_programs(2) - 1)
    def _():
        h2 = h1_sc[...] + acc_sc[...]
        gate = jax.nn.sigmoid(_dot(_rms(h2, pgn_ref[...]).astype(BF16), wpg_ref[...]))
        e = _dot(p_ref[0].astype(BF16), wpp_ref[...])
        o_ref[...] = h2 + _rms(e, ppn_ref[...]) * gate


def _post_mixer(h, mix, wo_bf16, p, nf, wffn_in_bf16, wffn_out_bf16, cw, cb, cst, pgn, wpg_bf16, wpp_bf16,
                ppn, *, p_layer, n_seq, stride, tm, tf):
    n, d = h.shape
    dff = wffn_out_bf16.shape[0]
    nj = dff // tf
    pad = cst.shape[1]
    ni = n // n_seq // tm
    n_mix = len(mix)
    row = lambda width: pl.BlockSpec((tm, width), lambda b, i, j: (b * ni + i, 0))
    full = lambda shape: pl.BlockSpec(shape, lambda b, i, j: (0, 0))
    in_specs = (
        [row(d)] + [row(a.shape[1]) for a in mix] + [full(w_.shape) for w_ in wo_bf16]
        + [
            pl.BlockSpec((1, tm, p.shape[2]), lambda b, i, j: (p_layer, b * ni + i, 0)),
            full((1, d)),
            pl.BlockSpec((d, tf), lambda b, i, j: (0, j)),
            pl.BlockSpec((d, tf), lambda b, i, j: (0, nj + j)),
            pl.BlockSpec((tf, d), lambda b, i, j: (j, 0)),
            pl.BlockSpec((FFN_CONV, tf), lambda b, i, j: (0, j)),
            pl.BlockSpec((1, tf), lambda b, i, j: (0, j)),
            pl.BlockSpec((1, pad, tf), lambda b, i, j: (b, 0, j)),
            full((1, d)),
            full(wpg_bf16.shape),
            full(wpp_bf16.shape),
            full((1, d)),
        ])
    out, fc = pl.pallas_call(
        functools.partial(_post_kernel, n_mix=n_mix, stride=stride, pad=pad),
        grid=(n_seq, ni, nj),
        in_specs=in_specs,
        out_specs=[
            pl.BlockSpec((tm, d), lambda b, i, j: (b * ni + i, 0)),
            pl.BlockSpec((1, nj, pad, tf), lambda b, i, j: (b, 0, 0, 0)),
        ],
        out_shape=[
            jax.ShapeDtypeStruct((n, d), F32),
            jax.ShapeDtypeStruct((n_seq, nj, pad, tf), F32),
        ],
        scratch_shapes=[
            pltpu.VMEM((tm, d), F32),
            pltpu.VMEM((tm, d), BF16),
            pltpu.VMEM((tm, d), F32),
            pltpu.VMEM((nj, pad, tf), F32),
        ],
        compiler_params=_params("arbitrary", "arbitrary", "arbitrary"),
        name="post_mixer",
    )(h, *mix, *wo_bf16, p, nf.reshape(1, d), wffn_in_bf16, wffn_in_bf16, wffn_out_bf16, cw,
      cb.reshape(1, dff), cst, pgn.reshape(1, d), wpg_bf16, wpp_bf16, ppn.reshape(1, d))
    return out, jnp.transpose(fc, (0, 2, 1, 3)).reshape(n_seq, pad, dff)


def _t5_bucket(dist):
    n = jnp.maximum(dist, 0)
    max_exact = REL_BUCKETS // 2
    nf = jnp.maximum(n, max_exact).astype(F32)
    large = max_exact + (jnp.log(nf / max_exact) / math.log(REL_MAX_DIST / max_exact)
                         * (REL_BUCKETS - max_exact)).astype(jnp.int32)
    large = jnp.minimum(large, REL_BUCKETS - 1)
    return jnp.where(n < max_exact, n, large)


def _bias_table(rel_bias, dist):
    onehot = (_t5_bucket(jnp.asarray(dist))[..., None] == jnp.arange(REL_BUCKETS)).astype(F32)
    return jnp.einsum('...b,bh->h...', onehot, rel_bias.astype(F32), precision=lax.Precision.HIGHEST)


def _top3(sc, idxf, n, axis=-1):
    picks = []
    for _ in range(MOBA_TOPK):
        m = jnp.max(sc, axis=axis, keepdims=True)
        first = jnp.min(jnp.where(sc == m, idxf, float(n)), axis=axis, keepdims=True)
        picks.append((first, m))
        sc = jnp.where(idxf == first, -jnp.inf, sc)
    return picks


def _moba_prompt_kernel(q_ref, k_ref, vt_ref, bo_ref, bp_ref, bf_ref, o_ref, km_sc, s0_sc, s1_sc):
    i = pl.program_id(2)
    nb = km_sc.shape[0]
    blk = MOBA_BLOCK
    hd = C_HEAD_DIM
    ext = hd - MASK_ROWS

    @pl.when(i == 0)
    def _():
        def body(jb, c):
            kb = k_ref[0, pl.ds(pl.multiple_of(jb * blk, blk), blk), :]
            km_sc[pl.ds(jb, 1), :] = jnp.sum(kb, axis=0, keepdims=True) * (1.0 / blk)
            return c
        lax.fori_loop(0, nb, body, 0)

    q_t = q_ref[0].T
    km = km_sc[...]
    km_lo = lax.broadcasted_iota(jnp.int32, (nb, LANES), 1) < hd
    rowf = lax.broadcasted_iota(jnp.int32, (nb, blk), 0).astype(F32)
    i_f = i.astype(F32)

    def mask_rows(kmh):
        sc = jnp.dot(kmh, q_t, precision=lax.Precision.HIGHEST, preferred_element_type=F32)
        sc = jnp.where(rowf < i_f, sc, -jnp.inf)
        sel = jnp.zeros((nb, blk), F32)
        for first, m in _top3(sc, rowf, nb, axis=0):
            sel = jnp.where((rowf == first) & (m > -jnp.inf), 1.0, sel)
        mask = jnp.where(sel > 0.0, 0.0, NEG)
        if nb < MASK_ROWS:
            mask = jnp.concatenate([mask, jnp.zeros((MASK_ROWS - nb, blk), F32)], axis=0)
        return mask

    def bias_rows(bias):
        hi = bias.astype(BF16).astype(F32)
        r = lax.broadcasted_iota(jnp.int32, (ext, blk), 0)
        return jnp.where(r == 0, hi, jnp.where(r == 1, bias - hi, 0.0))

    c = hd ** -0.5 * LOG2E
    qp_a = jnp.concatenate([q_t[:hd] * c, mask_rows(jnp.where(km_lo, km, 0.0)), bias_rows(bf_ref[0])],
                           axis=0).astype(BF16)
    qp_b = jnp.concatenate([mask_rows(jnp.where(km_lo, 0.0, km)), bias_rows(bf_ref[1]), q_t[hd:] * c],
                           axis=0).astype(BF16)

    def logits(jb, n_blk, block_ids, far):
        start = pl.multiple_of(jb * blk, blk)
        lane = lax.broadcasted_iota(jnp.int32, (1, LANES), 1)
        lo = lax.broadcasted_iota(jnp.int32, (blk, LANES), 1) < hd
        kbs_a, kbs_b = [], []
        for g in range(n_blk):
            kb = k_ref[0, pl.ds(pl.multiple_of(start + g * blk, blk), blk), :]
            e_a = jnp.zeros((1, LANES), F32)
            e_b = jnp.zeros((1, LANES), F32)
            if block_ids is not None:
                e_a = jnp.where(lane == hd + block_ids[g], 1.0, e_a)
                e_b = jnp.where(lane == block_ids[g], 1.0, e_b)
            if far:
                e_a = jnp.where((lane == hd + MASK_ROWS) | (lane == hd + MASK_ROWS + 1), 1.0, e_a)
                e_b = jnp.where((lane == MASK_ROWS) | (lane == MASK_ROWS + 1), 1.0, e_b)
            kbs_a.append(jnp.where(lo, kb, e_a).astype(BF16))
            kbs_b.append(jnp.where(lo, e_b, kb).astype(BF16))
        kb_a = jnp.concatenate(kbs_a, axis=0) if n_blk > 1 else kbs_a[0]
        kb_b = jnp.concatenate(kbs_b, axis=0) if n_blk > 1 else kbs_b[0]
        return _dot(kb_a, qp_a), _dot(kb_b, qp_b)

    def fold(carry, s_a, s_b, jb, n_blk):
        m_a, acc_a, m_b, acc_b = carry
        keys = n_blk * blk
        vt = vt_ref[0, :, pl.ds(pl.multiple_of(jb * blk, blk), keys)]
        ones = jnp.ones((PV_ONES, keys), F32)
        vt_a = jnp.concatenate([vt[:hd], ones], axis=0).astype(BF16)
        vt_b = jnp.concatenate([vt[hd:], ones], axis=0).astype(BF16)
        mn_a = jnp.maximum(m_a, jnp.max(s_a, axis=0, keepdims=True))
        mn_b = jnp.maximum(m_b, jnp.max(s_b, axis=0, keepdims=True))
        p_a = jnp.exp2(s_a - mn_a).astype(BF16)
        p_b = jnp.exp2(s_b - mn_b).astype(BF16)
        acc_a = jnp.exp2(m_a - mn_a) * acc_a + _dot(vt_a, p_a)
        acc_b = jnp.exp2(m_b - mn_b) * acc_b + _dot(vt_b, p_b)
        return mn_a, acc_a, mn_b, acc_b

    m0 = jnp.full((1, blk), NEG, F32)
    acc0 = jnp.zeros((hd + PV_ONES, blk), F32)
    carry = (m0, acc0, m0, acc0)
    no_mask = -LANES
    first = i == 0
    near = jnp.maximum(i - 1, 0)
    s_a, s_b = logits(near, GROUP, [jnp.where(first, no_mask, i - 1), jnp.where(first, nb - 1, no_mask)], False)
    s0_sc[0] = s_a + jnp.concatenate([jnp.where(first, bo_ref[0], bp_ref[0]), bo_ref[0]], axis=0)
    s0_sc[1] = s_b + jnp.concatenate([jnp.where(first, bo_ref[1], bp_ref[1]), bo_ref[1]], axis=0)

    n_far = jnp.maximum(i - 1, 0)
    n_grp = (n_far + GROUP - 1) // GROUP

    def far_logits(grp, s_sc):
        jb = jnp.minimum(grp * GROUP, nb - GROUP)
        ids = [jnp.where(grp * GROUP + g < n_far, grp * GROUP + g, nb - 1) for g in range(GROUP)]
        s_sc[0], s_sc[1] = logits(jb, GROUP, ids, True)

    def far_fold(cy, grp, s_sc):
        return fold(cy, s_sc[0], s_sc[1], jnp.minimum(grp * GROUP, nb - GROUP), GROUP)

    far_logits(0, s1_sc)
    carry = fold(carry, s0_sc[0], s0_sc[1], near, GROUP)

    def far_pair(it, cy):
        far_logits(2 * it + 1, s0_sc)
        cy = far_fold(cy, 2 * it, s1_sc)
        far_logits(2 * it + 2, s1_sc)
        return far_fold(cy, 2 * it + 1, s0_sc)

    m_a, acc_a, m_b, acc_b = lax.fori_loop(0, (n_grp + 1) // 2, far_pair, carry)
    out_t = jnp.concatenate([acc_a[:hd] / acc_a[hd:hd + 1], acc_b[:hd] / acc_b[hd:hd + 1]], axis=0)
    o_ref[0] = out_t.T


def _moba_prompt(q, k, v_t, rel_bias):
    b, t, width = q.shape
    blk = MOBA_BLOCK
    nb = t // blk
    assert nb <= MASK_ROWS
    hp = width // LANES
    ll = np.arange(blk)[:, None]
    ii = np.arange(blk)[None, :]
    bias_own = jnp.where(ii >= ll, _bias_table(rel_bias, np.maximum(ii - ll, 0)) * LOG2E, NEG)
    bias_prev = _bias_table(rel_bias, blk + ii - ll) * LOG2E
    bias_far = (_bias_table(rel_bias, np.array([2 * blk])) * LOG2E).reshape(-1, 1, 1)
    pair = lambda shape: pl.BlockSpec((2,) + shape, lambda bi, hi, qi: (hi, 0, 0))
    return pl.pallas_call(
        _moba_prompt_kernel,
        grid=(b, hp, nb),
        in_specs=[
            pl.BlockSpec((1, blk, LANES), lambda bi, hi, qi: (bi, qi, hi)),
            pl.BlockSpec((1, t, LANES), lambda bi, hi, qi: (bi, 0, hi)),
            pl.BlockSpec((1, LANES, t), lambda bi, hi, qi: (bi, hi, 0)),
            pair((blk, blk)), pair((blk, blk)), pair((1, 1)),
        ],
        out_specs=pl.BlockSpec((1, blk, LANES), lambda bi, hi, qi: (bi, qi, hi)),
        out_shape=jax.ShapeDtypeStruct((b, t, width), F32),
        scratch_shapes=[pltpu.VMEM((nb, LANES), F32),
                        pltpu.VMEM((2, GROUP * blk, blk), F32),
                        pltpu.VMEM((2, GROUP * blk, blk), F32)],
        compiler_params=_params("arbitrary", "arbitrary", "arbitrary"),
        name="moba_prompt",
    )(q, k, v_t, bias_own, bias_prev, bias_far)


def _kmean_kernel(pt_ref, *refs):
    page_refs, o_ref = refs[:-1], refs[-1]
    j = pl.program_id(1)

    @pl.when(j == 0)
    def _():
        o_ref[...] = jnp.zeros(o_ref.shape, F32)

    lane = lax.broadcasted_iota(jnp.int32, o_ref.shape[1:], 2)
    out = o_ref[0]
    for g in range(len(page_refs) // PAGES_PER_BLOCK):
        pages = page_refs[g * PAGES_PER_BLOCK:(g + 1) * PAGES_PER_BLOCK]
        tot = pages[0][0]
        for pr in pages[1:]:
            tot = tot + pr[0]
        col = jnp.sum(tot, axis=-1, keepdims=True) * (1.0 / MOBA_BLOCK)
        out = jnp.where(lane == j * KMEAN_BLOCKS_PER_STEP + g, col, out)
    o_ref[0] = out


def _block_means(ck_t, page_table, nb):
    _, h, dh, page = ck_t.shape
    b = page_table.shape[0]
    g = KMEAN_BLOCKS_PER_STEP
    assert nb % g == 0
    n_pages = g * PAGES_PER_BLOCK
    page_spec = lambda k: pl.BlockSpec((1, h, dh, page), lambda bi, j, pt: (pt[bi, n_pages * j + k], 0, 0, 0))
    return pl.pallas_call(
        _kmean_kernel,
        grid_spec=pltpu.PrefetchScalarGridSpec(
            num_scalar_prefetch=1,
            grid=(b, nb // g),
            in_specs=[page_spec(k) for k in range(n_pages)],
            out_specs=pl.BlockSpec((1, h, dh, nb), lambda bi, j, pt: (bi, 0, 0, 0)),
        ),
        out_shape=jax.ShapeDtypeStruct((b, h, dh, nb), F32),
        compiler_params=_params("arbitrary", "arbitrary"),
        name="moba_block_means",
    )(page_table, *([ck_t] * n_pages))


def _select_kernel(qt_ref, km_ref, o_ref):
    s_len, _, h = qt_ref.shape[1:]
    nb = km_ref.shape[3]
    rows = s_len * h
    rid = lax.broadcasted_iota(jnp.int32, (rows, nb), 0)
    sc = jnp.zeros((rows, nb), F32)
    for s in range(s_len):
        qt = qt_ref[0, s]
        for hh in range(h):
            sc = jnp.where(rid == s * h + hh, jnp.sum(km_ref[0, hh] * qt[:, hh:hh + 1], axis=0, keepdims=True), sc)
    colf = lax.broadcasted_iota(jnp.int32, (rows, nb), 1).astype(F32)
    picks = [jnp.minimum(first, nb - 1.0).astype(jnp.int32) for first, _ in _top3(sc, colf, nb)]
    lane = lax.broadcasted_iota(jnp.int32, (rows, LANES), 1)
    o_ref[0] = jnp.where(lane == 0, picks[0], jnp.where(lane == 1, picks[1], picks[2]))


def _select_blocks(q_t, kmean_t):
    b, s_len, dh, h = q_t.shape
    nb = kmean_t.shape[3]
    out = pl.pallas_call(
        _select_kernel,
        grid=(b,),
        in_specs=[
            pl.BlockSpec((1, s_len, dh, h), lambda bi: (bi, 0, 0, 0)),
            pl.BlockSpec((1, h, dh, nb), lambda bi: (bi, 0, 0, 0)),
        ],
        out_specs=pl.BlockSpec((1, s_len * h, LANES), lambda bi: (bi, 0, 0)),
        out_shape=jax.ShapeDtypeStruct((b, s_len * h, LANES), jnp.int32),
        compiler_params=_params("arbitrary"),
        name="moba_select",
    )(q_t, kmean_t)
    return out[:, :, :MOBA_TOPK].reshape(b, s_len, h, MOBA_TOPK)


def _moba_sample_kernel(pt_ref, sel_ref, ck_ref, cv_ref, qt_ref, kn_ref, vn_ref, bl_ref, bo_ref, bf_ref,
                        o_ref, kbuf, vbuf, sem, *, nb):
    b = pl.program_id(0)
    s = pl.program_id(1)
    s_len = pl.num_programs(1)
    h, _, dh, _ = kbuf.shape
    ppb = PAGES_PER_BLOCK

    def copies(hh, n, pg):
        idx = sel_ref[((b * s_len + s) * h + hh) * MOBA_TOPK + n]
        phys = pt_ref[b, idx * ppb + pg]
        slot = n * ppb + pg
        return (pltpu.make_async_copy(ck_ref.at[phys, hh], kbuf.at[hh, slot], sem.at[0, hh, slot]),
                pltpu.make_async_copy(cv_ref.at[phys, hh], vbuf.at[hh, slot], sem.at[1, hh, slot]))

    slots = [(hh, n, pg) for hh in range(h) for n in range(MOBA_TOPK) for pg in range(ppb)]
    for sl in slots:
        for cp in copies(*sl):
            cp.start()
    for sl in slots:
        for cp in copies(*sl):
            cp.wait()

    lane = lax.broadcasted_iota(jnp.int32, (dh, LANES), 1)
    qt = qt_ref[0, 0] * (dh ** -0.5)
    out = jnp.zeros((dh, LANES), F32)
    for hh in range(h):
        qc = qt[:, hh:hh + 1]
        far = bf_ref[hh:hh + 1, :]
        logits = []
        for n in range(MOBA_TOPK):
            idx = sel_ref[((b * s_len + s) * h + hh) * MOBA_TOPK + n]
            for pg in range(ppb):
                lg = jnp.sum(kbuf[hh, n * ppb + pg] * qc, axis=0, keepdims=True)
                logits.append(lg + jnp.where(idx == nb - 1, bl_ref[0, hh, pg:pg + 1, :], far))
        lown = jnp.sum(kn_ref[0, hh] * qc, axis=0, keepdims=True) + bo_ref[0, hh:hh + 1, :]
        m = jnp.max(lown, axis=-1, keepdims=True)
        for lg in logits:
            m = jnp.maximum(m, jnp.max(lg, axis=-1, keepdims=True))
        pown = jnp.exp(lown - m)
        den = jnp.sum(pown, axis=-1, keepdims=True)
        col = jnp.sum(vn_ref[0, hh] * pown, axis=-1, keepdims=True)
        for slot, lg in enumerate(logits):
            p = jnp.exp(lg - m)
            den = den + jnp.sum(p, axis=-1, keepdims=True)
            col = col + jnp.sum(vbuf[hh, slot] * p, axis=-1, keepdims=True)
        out = jnp.where(lane == hh, col / den, out)
    o_ref[0, 0] = out


def _moba_sample(q, k_new, v_new, ck_t, cv_t, page_table, rel_bias, past_len):
    b, s_len, h, dh = q.shape
    page = ck_t.shape[3]
    blk = MOBA_BLOCK
    assert past_len % blk == 0 and blk == PAGES_PER_BLOCK * page
    nb = past_len // blk
    assert nb >= MOBA_TOPK
    q_t = jnp.transpose(q, (0, 1, 3, 2))
    kmean_t = _block_means(ck_t, page_table, nb)
    sel = _select_blocks(q_t, kmean_t)

    ss = np.arange(s_len)
    bias_last = _bias_table(rel_bias, blk + ss[:, None] - np.arange(blk)[None, :])
    bias_last = jnp.transpose(bias_last, (1, 0, 2)).reshape(s_len, h, PAGES_PER_BLOCK, page)
    own_d = ss[:, None] - ss[None, :]
    bias_own = jnp.where(own_d >= 0, _bias_table(rel_bias, np.maximum(own_d, 0)), NEG)
    bias_own = jnp.transpose(bias_own, (1, 0, 2))
    bias_far = _bias_table(rel_bias, np.array([2 * blk]))
    kn_t = jnp.transpose(k_new, (0, 2, 3, 1))
    vn_t = jnp.transpose(v_new, (0, 2, 3, 1))

    out = pl.pallas_call(
        functools.partial(_moba_sample_kernel, nb=nb),
        grid_spec=pltpu.PrefetchScalarGridSpec(
            num_scalar_prefetch=2,
            grid=(b, s_len),
            in_specs=[
                pl.BlockSpec(memory_space=pl.ANY),
                pl.BlockSpec(memory_space=pl.ANY),
                pl.BlockSpec((1, 1, dh, h), lambda bi, si, pt, sl: (bi, si, 0, 0)),
                pl.BlockSpec((1, h, dh, s_len), lambda bi, si, pt, sl: (bi, 0, 0, 0)),
                pl.BlockSpec((1, h, dh, s_len), lambda bi, si, pt, sl: (bi, 0, 0, 0)),
                pl.BlockSpec((1, h, PAGES_PER_BLOCK, page), lambda bi, si, pt, sl: (si, 0, 0, 0)),
                pl.BlockSpec((1, h, s_len), lambda bi, si, pt, sl: (si, 0, 0)),
                pl.BlockSpec((h, 1), lambda bi, si, pt, sl: (0, 0)),
            ],
            out_specs=pl.BlockSpec((1, 1, dh, LANES), lambda bi, si, pt, sl: (bi, si, 0, 0)),
            scratch_shapes=[
                pltpu.VMEM((h, PAGES_PER_BLOCK * MOBA_TOPK, dh, page), F32),
                pltpu.VMEM((h, PAGES_PER_BLOCK * MOBA_TOPK, dh, page), F32),
                pltpu.SemaphoreType.DMA((2, h, PAGES_PER_BLOCK * MOBA_TOPK)),
            ],
        ),
        out_shape=jax.ShapeDtypeStruct((b, s_len, dh, LANES), F32),
        compiler_params=_params("arbitrary", "arbitrary"),
        name="moba_sample",
    )(page_table, sel.reshape(-1), ck_t, cv_t, q_t, kn_t, vn_t, bias_last, bias_own, bias_far)
    return jnp.transpose(out[:, :, :, :h], (0, 1, 3, 2))


def _prep_weights(w):
    bf = lambda a: a.astype(BF16)
    n_even = w['w0_in'].shape[0]
    n_odd = w['c_in'].shape[0]
    out = dict(w)
    out['w0_in'] = bf(w['w0_in'])
    out['w0_out'] = bf(w['w0_out'])
    out['lru_wg'] = [bf(jnp.concatenate([_block_diag(w['lru_wa'][e]), _block_diag(w['lru_wi'][e])], axis=1))
                     for e in range(n_even)]
    out['c_in'] = bf(w['c_in'])
    out['c_out'] = bf(w['c_out'])
    out['c_gain'] = [jnp.concatenate([jnp.tile(w['c_qn'][o], C_HEADS), jnp.tile(w['c_kn'][o], C_HEADS),
                                      jnp.ones((C_HEADS * C_HEAD_DIM,), F32)]).reshape(1, -1)
                     for o in range(n_odd)]
    out['ffn_in'] = bf(w['ffn_in'])
    out['ffn_out'] = bf(w['ffn_out'])
    out['ple_proj'] = bf(w['ple_proj'])
    out['ple_gate'] = bf(w['ple_gate'])
    return out


def _post(w, l, h, mix, wo, p, p_layer, cst, *, n_seq, stride, tm, tf):
    return _post_mixer(h, mix, wo, p, w['norm_ffn'][l], w['ffn_in'][l], w['ffn_out'][l], w['ffn_conv_w'][l],
                       w['ffn_conv_b'][l], cst, w['ple_gate_norm'][l], w['ple_gate'][l], w['ple_proj'][l],
                       w['ple_post_norm'][l], p_layer=p_layer, n_seq=n_seq, stride=stride, tm=tm, tf=tf)


def _prompt_trunk(x, p, w, rel_bias):
    b, t, d = x.shape
    n = b * t
    depth = p.shape[0]
    h = x.reshape(n, d)
    tm = 512
    rets, lrus, lconvs, fconvs, ks, vs = [], [], [], [], [], []
    lw = w['lru_conv_w'].shape[2]
    dff = w['ffn_out'].shape[1]
    for l in range(depth):
        cst = jnp.zeros((b, SUBLANES, dff), F32)
        if l % 2 == 0:
            e = l // 2
            proj = _norm_proj(h, w['norm_mix'][l], w['w0_in'][e], tm=512, tn=1024).reshape(b, t, -1)
            o, s_new = _retention(proj, jnp.zeros((b, RET_HEADS, RET_DIM, RET_DIM), F32), jnp.arange(t),
                                  w['ret_gn_g'][e], w['ret_gn_b'][e], c_eff=RET_CHUNK, c_pad=RET_CHUNK,
                                  rows_per_step=512)
            ob, h_last, tail = _lru(proj, jnp.zeros((b, SUBLANES, lw), F32), jnp.zeros((b, 1, lw), F32),
                                    w['lru_conv_w'][e], w['lru_conv_b'][e], w['lru_wg'][e], w['lru_ba'][e],
                                    w['lru_bi'][e], w['lru_lam'][e], stride=1, rows_per_step=512)
            rets.append(s_new)
            lrus.append(h_last[:, 0])
            lconvs.append(tail[:, SUBLANES - (LRU_CONV - 1):])
            rw = o.shape[-1]
            mix = [o.reshape(n, rw), ob.reshape(n, lw)]
            wo = [w['w0_out'][e][:rw], w['w0_out'][e][rw:]]
        else:
            o_ = l // 2
            q, k, k_t, v_t = _norm_proj(h, w['norm_mix'][l], w['c_in'][o_], tm=512, tn=1024,
                                        head_gain=w['c_gain'][o_], n_rms_tiles=2,
                                        plan=((False,), (False, True), (True,)), n_seq=b)
            cw = q.shape[1]
            att = _moba_prompt(q.reshape(b, t, cw), k.reshape(b, t, cw), v_t, rel_bias)
            heads = lambda a: jnp.transpose(a.reshape(b, C_HEADS, C_HEAD_DIM, t), (0, 3, 1, 2))
            ks.append(heads(k_t))
            vs.append(heads(v_t))
            mix = [att.reshape(n, cw)]
            wo = [w['c_out'][o_]]
        h, fc = _post(w, l, h, mix, wo, p.reshape(depth, n, -1), l, cst, n_seq=b, stride=1, tm=tm, tf=512)
        fconvs.append(fc[:, SUBLANES - (FFN_CONV - 1):])
    return (h.reshape(b, t, d), jnp.stack(rets), jnp.stack(lrus), jnp.stack(lconvs), jnp.stack(fconvs),
            jnp.stack(ks), jnp.stack(vs))


def _sample_trunk(x, p, ret0, lru0, lconv0, fconv0, cache_k, cache_v, page_table, w, rel_bias):
    b, t, d = x.shape
    n = b * t
    depth = p.shape[0]
    past_len = page_table.shape[1] * cache_k.shape[2]
    tmaj = lambda a: jnp.transpose(a, (1, 0) + tuple(range(2, a.ndim)))
    h = tmaj(x).reshape(n, d)
    rets, lrus, lconvs, fconvs, ks, vs = [], [], [], [], [], []
    lw = w['lru_conv_w'].shape[2]
    for l in range(depth):
        cst = tmaj(fconv0[l]).reshape(1, (FFN_CONV - 1) * b, -1)
        if l % 2 == 0:
            e = l // 2
            proj = _norm_proj(h, w['norm_mix'][l], w['w0_in'][e], tm=n, tn=1024)
            rcols = 4 * RET_HEADS * RET_DIM
            pr = tmaj(proj[:, :rcols].reshape(t, b, rcols))
            pr = jnp.pad(pr, ((0, 0), (0, SUBLANES - t), (0, 0)))
            pos = past_len + jnp.arange(SUBLANES)
            o, s_new = _retention(pr, ret0[e].astype(F32), pos, w['ret_gn_g'][e], w['ret_gn_b'][e],
                                  c_eff=t, c_pad=SUBLANES, rows_per_step=SUBLANES)
            o = tmaj(o[:, :t]).reshape(n, -1)
            tail0 = tmaj(lconv0[e]).reshape(1, (LRU_CONV - 1) * b, lw)
            ob, h_last, tail = _lru(proj.reshape(1, n, -1), tail0, lru0[e].reshape(1, b, lw),
                                    w['lru_conv_w'][e], w['lru_conv_b'][e], w['lru_wg'][e], w['lru_ba'][e],
                                    w['lru_bi'][e], w['lru_lam'][e], stride=b, rows_per_step=n)
            rets.append(s_new)
            lrus.append(h_last[0])
            lconvs.append(tmaj(tail.reshape(LRU_CONV - 1, b, lw)))
            rw = o.shape[-1]
            mix = [o, ob.reshape(n, lw)]
            wo = [w['w0_out'][e][:rw], w['w0_out'][e][rw:]]
        else:
            o_ = l // 2
            q, k, v = _norm_proj(h, w['norm_mix'][l], w['c_in'][o_], tm=n, tn=1024, head_gain=w['c_gain'][o_],
                                 n_rms_tiles=2, plan=((False,), (False,), (False,)))
            heads = lambda a: tmaj(a.reshape(t, b, C_HEADS, C_HEAD_DIM))
            q, k, v = heads(q), heads(k), heads(v)
            ck_t = jnp.transpose(cache_k[o_], (0, 2, 3, 1))
            cv_t = jnp.transpose(cache_v[o_], (0, 2, 3, 1))
            att = _moba_sample(q, k, v, ck_t, cv_t, page_table, rel_bias, past_len)
            ks.append(k)
            vs.append(v)
            mix = [tmaj(att).reshape(n, -1)]
            wo = [w['c_out'][o_]]
        h, fc = _post(w, l, h, mix, wo, tmaj(p[l]).reshape(1, n, -1), 0, cst, n_seq=1, stride=b, tm=n, tf=512)
        fconvs.append(tmaj(fc.reshape(FFN_CONV - 1, b, -1)))
    return (tmaj(h.reshape(t, b, d)), jnp.stack(rets), jnp.stack(lrus), jnp.stack(lconvs), jnp.stack(fconvs),
            jnp.stack(ks), jnp.stack(vs))


def kernel(x_prompt, x_sample, state_ret, state_lru, state_lru_conv, state_ffn_conv, cache_k, cache_v, page_table, p_prompt, p_sample, norm_mix, norm_ffn, w0_in, ret_gn_g, ret_gn_b, lru_conv_w, lru_conv_b, lru_wa, lru_ba, lru_wi, lru_bi, lru_lam, w0_out, c_in, c_qn, c_kn, c_out, rel_bias, ffn_in, ffn_conv_w, ffn_conv_b, ffn_out, ple_proj, ple_gate_norm, ple_gate, ple_post_norm):
    w = _prep_weights(dict(
        norm_mix=norm_mix, norm_ffn=norm_ffn, w0_in=w0_in, ret_gn_g=ret_gn_g, ret_gn_b=ret_gn_b,
        lru_conv_w=lru_conv_w, lru_conv_b=lru_conv_b, lru_wa=lru_wa, lru_ba=lru_ba, lru_wi=lru_wi,
        lru_bi=lru_bi, lru_lam=lru_lam, w0_out=w0_out, c_in=c_in, c_qn=c_qn, c_kn=c_kn, c_out=c_out,
        ffn_in=ffn_in, ffn_conv_w=ffn_conv_w, ffn_conv_b=ffn_conv_b, ffn_out=ffn_out, ple_proj=ple_proj,
        ple_gate_norm=ple_gate_norm, ple_gate=ple_gate, ple_post_norm=ple_post_norm))
    y_p, ret_p, lru_p, lconv_p, fconv_p, k_p, v_p = _prompt_trunk(x_prompt, p_prompt, w, rel_bias)
    y_s, ret_s, lru_s, lconv_s, fconv_s, k_s, v_s = _sample_trunk(
        x_sample, p_sample, state_ret, state_lru, state_lru_conv, state_ffn_conv, cache_k, cache_v, page_table,
        w, rel_bias)
    return (y_p, y_s, ret_p, ret_s, lru_p, lru_s, lconv_p, lconv_s, fconv_p, fconv_s, k_p, k_s, v_p, v_s)
```

```python
import functools
import math

import numpy as np
import jax
import jax.numpy as jnp
from jax import lax
from jax.experimental import pallas as pl
from jax.experimental.pallas import tpu as pltpu

F32 = jnp.float32
BF16 = jnp.bfloat16
EPS = 1e-6
NEG = -1e30

LANES = 128
SUBLANES = 8
MXU_WIDTH = 256
VMEM_LIMIT_BYTES = 48 * 1024 * 1024

RET_HEADS = 4
RET_DIM = 128
RET_CHUNK = 128
ROPE_THETA = 10000.0
LRU_C = 8.0
LRU_CONV = 4
FFN_CONV = 3
C_HEADS = 16
C_HEAD_DIM = 64
MOBA_BLOCK = 256
MOBA_TOPK = 3
REL_BUCKETS = 32
REL_MAX_DIST = 128
LOG2E = 1.4426950408889634
MASK_ROWS = 32
PV_ONES = 16
GROUP = 2
KMEAN_BLOCKS_PER_STEP = 4
PAGES_PER_BLOCK = 2


def _params(*sem):
    return pltpu.CompilerParams(dimension_semantics=sem, vmem_limit_bytes=VMEM_LIMIT_BYTES)


def _rms(x, g):
    return x * lax.rsqrt(jnp.mean(x * x, axis=-1, keepdims=True) + EPS) * g


def _gelu(x):
    return 0.5 * x * (1.0 + jnp.tanh(0.7978845608028654 * (x + 0.044715 * (x * x * x))))


def _dot(a, b):
    return jnp.dot(a, b, preferred_element_type=F32)


def _dot_nt(a, b):
    return lax.dot_general(a, b, (((1,), (1,)), ((), ())), preferred_element_type=F32)


def _dot_tn(a, b):
    return lax.dot_general(a, b, (((0,), (0,)), ((), ())), preferred_element_type=F32)


def _head_rms(y, hd):
    tm, tn = y.shape
    lo = lax.broadcasted_iota(jnp.int32, (tm, LANES), 1) < hd
    outs = []
    for s in range(tn // LANES):
        ys = y[:, s * LANES:(s + 1) * LANES]
        y2 = ys * ys
        s0 = jnp.sum(jnp.where(lo, y2, 0.0), axis=-1, keepdims=True)
        s1 = jnp.sum(jnp.where(lo, 0.0, y2), axis=-1, keepdims=True)
        r = lax.rsqrt(jnp.where(lo, s0, s1) * (1.0 / hd) + EPS)
        outs.append(ys * r)
    return jnp.concatenate(outs, axis=-1)


def _norm_proj_kernel(x_ref, g_ref, w_ref, hg_ref, *o_refs, n_rms_tiles, plan, tn):
    nout = w_ref.shape[1]
    xn = _rms(x_ref[...], g_ref[...]).astype(BF16)
    k = 0
    for t in range(nout // tn):
        cols = slice(t * tn, (t + 1) * tn)
        y = _dot(xn, w_ref[:, cols])
        if not plan:
            o_refs[0][:, cols] = y
            continue
        if t < n_rms_tiles:
            y = _head_rms(y, C_HEAD_DIM) * hg_ref[:, cols]
        for transposed in plan[t]:
            if transposed:
                o_refs[k][0] = y.T
            else:
                o_refs[k][...] = y
            k += 1


def _norm_proj(x, g, w_bf16, *, tm, tn, head_gain=None, n_rms_tiles=0, plan=None, n_seq=1):
    n, d = x.shape
    nout = w_bf16.shape[1]
    nj = nout // tn
    if head_gain is None:
        head_gain = jnp.ones((1, nout), F32)
    if plan is None:
        out_shape = [jax.ShapeDtypeStruct((n, nout), F32)]
        out_specs = [pl.BlockSpec((tm, nout), lambda i: (i, 0))]
    else:
        assert len(plan) == nj
        t_len = n // n_seq
        nt = t_len // tm
        out_shape, out_specs = [], []
        for outs in plan:
            for transposed in outs:
                if transposed:
                    out_shape.append(jax.ShapeDtypeStruct((n_seq, tn, t_len), F32))
                    out_specs.append(pl.BlockSpec((1, tn, tm), lambda i: (i // nt, 0, i % nt)))
                else:
                    out_shape.append(jax.ShapeDtypeStruct((n, tn), F32))
                    out_specs.append(pl.BlockSpec((tm, tn), lambda i: (i, 0)))
    outs = pl.pallas_call(
        functools.partial(_norm_proj_kernel, n_rms_tiles=n_rms_tiles, plan=plan, tn=tn),
        grid=(n // tm,),
        in_specs=[
            pl.BlockSpec((tm, d), lambda i: (i, 0)),
            pl.BlockSpec((1, d), lambda i: (0, 0)),
            pl.BlockSpec((d, nout), lambda i: (0, 0)),
            pl.BlockSpec((1, nout), lambda i: (0, 0)),
        ],
        out_specs=out_specs,
        out_shape=out_shape,
        compiler_params=_params("arbitrary"),
        name="norm_proj",
    )(x, g.reshape(1, d), w_bf16, head_gain)
    return outs[0] if plan is None else outs


def _ret_kernel(q_ref, k_ref, v_ref, g_ref, s0_ref, cos_ref, sin_ref, dm_ref, qd_ref, kd_ref, cd_ref,
                gg_ref, gb_ref, o_ref, s_ref, st_ref, *, c, nsub):
    t = pl.program_id(2)

    @pl.when(t == 0)
    def _():
        st_ref[...] = s0_ref[0, 0]

    dm = dm_ref[0]
    qd = qd_ref[0]
    kd = kd_ref[0]
    cd = cd_ref[0]
    gg = gg_ref[...]
    gb = gb_ref[...]
    for u in range(nsub):
        rows = slice(u * c, (u + 1) * c)
        cos = cos_ref[rows, :]
        sin = sin_ref[rows, :]
        q = q_ref[0, rows, :]
        k = k_ref[0, rows, :]
        q = q * cos + pltpu.roll(q, RET_DIM // 2, 1) * sin
        k = (k * cos + pltpu.roll(k, RET_DIM // 2, 1) * sin) * (RET_DIM ** -0.5)
        s = st_ref[...]
        qb = q.astype(BF16)
        vb = v_ref[0, rows, :].astype(BF16)
        att = _dot_nt(qb, k.astype(BF16)) * dm
        o = _dot(att.astype(BF16), vb) + _dot(qb, s.astype(BF16)) * qd
        st_ref[...] = s * cd + _dot_tn((k * kd).astype(BF16), vb)
        mu = jnp.mean(o, axis=-1, keepdims=True)
        oc = o - mu
        var = jnp.mean(oc * oc, axis=-1, keepdims=True)
        o = oc * lax.rsqrt(var + EPS) * gg + gb
        gate = g_ref[0, rows, :]
        o_ref[0, rows, :] = o * (gate * jax.nn.sigmoid(gate))

    @pl.when(t == pl.num_programs(2) - 1)
    def _():
        s_ref[0, 0] = st_ref[...]


def _retention(proj, s0, pos, gn_g, gn_b, *, c_eff, c_pad, rows_per_step):
    b, t, _ = proj.shape
    h = RET_HEADS
    half = RET_DIM // 2
    inv = ROPE_THETA ** (-jnp.arange(half, dtype=F32) / half)
    ang = pos.astype(F32)[:, None] * inv[None, :]
    cos2 = jnp.concatenate([jnp.cos(ang), jnp.cos(ang)], axis=-1)
    sin2 = jnp.concatenate([-jnp.sin(ang), jnp.sin(ang)], axis=-1)
    log_g = jnp.log1p(-jnp.exp2(-5.0 - jnp.arange(h, dtype=F32)))
    idx = jnp.arange(c_pad, dtype=F32)
    diff = idx[:, None] - idx[None, :]
    dmask = jnp.where(diff >= 0, jnp.exp(log_g[:, None, None] * jnp.maximum(diff, 0.0)), 0.0)
    q_dec = jnp.exp(log_g[:, None, None] * (idx[None, :, None] + 1.0))
    k_dec = jnp.exp(log_g[:, None, None] * (c_eff - 1.0 - idx[None, :, None]))
    c_dec = jnp.exp(log_g * c_eff).reshape(h, 1, 1)
    r = rows_per_step
    nsub = r // c_pad
    col = lambda off: pl.BlockSpec((1, r, RET_DIM), lambda bi, hi, ti, off=off: (bi, ti, off + hi))
    per_head = lambda shape: pl.BlockSpec((1,) + shape, lambda bi, hi, ti: (hi, 0, 0))
    o, s_new = pl.pallas_call(
        functools.partial(_ret_kernel, c=c_pad, nsub=nsub),
        grid=(b, h, t // r),
        in_specs=[
            col(0), col(h), col(2 * h), col(3 * h),
            pl.BlockSpec((1, 1, RET_DIM, RET_DIM), lambda bi, hi, ti: (bi, hi, 0, 0)),
            pl.BlockSpec((r, RET_DIM), lambda bi, hi, ti: (ti, 0)),
            pl.BlockSpec((r, RET_DIM), lambda bi, hi, ti: (ti, 0)),
            per_head((c_pad, c_pad)), per_head((c_pad, 1)), per_head((c_pad, 1)), per_head((1, 1)),
            pl.BlockSpec((1, RET_DIM), lambda bi, hi, ti: (0, hi)),
            pl.BlockSpec((1, RET_DIM), lambda bi, hi, ti: (0, hi)),
        ],
        out_specs=[
            pl.BlockSpec((1, r, RET_DIM), lambda bi, hi, ti: (bi, ti, hi)),
            pl.BlockSpec((1, 1, RET_DIM, RET_DIM), lambda bi, hi, ti: (bi, hi, 0, 0)),
        ],
        out_shape=[
            jax.ShapeDtypeStruct((b, t, h * RET_DIM), F32),
            jax.ShapeDtypeStruct((b, h, RET_DIM, RET_DIM), F32),
        ],
        scratch_shapes=[pltpu.VMEM((RET_DIM, RET_DIM), F32)],
        compiler_params=_params("arbitrary", "arbitrary", "arbitrary"),
        name="retention",
    )(proj, proj, proj, proj, s0, cos2, sin2, dmask, q_dec, k_dec, c_dec,
      gn_g.reshape(1, -1), gn_b.reshape(1, -1))
    return o, s_new


def _shift_rows(x, shift, fill):
    rows = lax.broadcasted_iota(jnp.int32, x.shape, 0)
    return jnp.where(rows >= shift, pltpu.roll(x, shift, 0), fill)


def _lru_kernel(bx_ref, bg_ref, tail0_ref, h0_ref, cw_ref, cb_ref, wg_ref, ba_ref, bi_ref, lam_ref,
                ob_ref, hl_ref, tail_ref, tail_sc, h_sc, *, stride, pad):
    t = pl.program_id(1)
    r, w = bx_ref.shape[1], bx_ref.shape[2]

    @pl.when(t == 0)
    def _():
        tail_sc[...] = tail0_ref[0]
        h_sc[...] = h0_ref[0]

    bx = bx_ref[0]
    xp = jnp.concatenate([tail_sc[...], bx], axis=0)
    xc = cb_ref[...]
    for j in range(LRU_CONV):
        off = pad - (LRU_CONV - 1 - j) * stride
        xc = xc + xp[off:off + r] * cw_ref[j:j + 1, :]
    new_tail = xp[r:r + pad]
    tail_sc[...] = new_tail
    tail_ref[0] = new_tail

    z = _dot(xc.astype(BF16), wg_ref[...])
    rg = jax.nn.sigmoid(z[:, :w] + ba_ref[...])
    ig = jax.nn.sigmoid(z[:, w:] + bi_ref[...])
    nl = -lam_ref[...]
    softplus = jnp.maximum(nl, 0.0) + jnp.log1p(jnp.exp(-jnp.abs(nl)))
    log_a = -LRU_C * rg * softplus
    a = jnp.exp(log_a)
    th = jnp.tanh(log_a)
    u = jnp.sqrt(-2.0 * th / (1.0 - th)) * (ig * xc)

    d = 1
    while d < r // stride:
        u = u + a * _shift_rows(u, d * stride, 0.0)
        a = a * _shift_rows(a, d * stride, 1.0)
        d *= 2
    hc = h_sc[...]
    if r > stride:
        hc = jnp.tile(hc, (r // stride, 1)) if stride > 1 else jnp.broadcast_to(hc, (r, w))
    hs = u + a * hc
    h_last = hs[r - stride:, :]
    h_sc[...] = h_last
    hl_ref[0] = h_last
    ob_ref[0] = hs * _gelu(bg_ref[0])


def _lru(proj, tail0, h0, cw, cb, wg_bf16, ba, bi, lam, *, stride, rows_per_step):
    b, t, _ = proj.shape
    w = cw.shape[1]
    pad = tail0.shape[1]
    r = rows_per_step
    vec = lambda: pl.BlockSpec((1, w), lambda bi_, ti: (0, 0))
    ob, h_last, tail = pl.pallas_call(
        functools.partial(_lru_kernel, stride=stride, pad=pad),
        grid=(b, t // r),
        in_specs=[
            pl.BlockSpec((1, r, w), lambda bi_, ti: (bi_, ti, 4)),
            pl.BlockSpec((1, r, w), lambda bi_, ti: (bi_, ti, 5)),
            pl.BlockSpec((1, pad, w), lambda bi_, ti: (bi_, 0, 0)),
            pl.BlockSpec((1, stride, w), lambda bi_, ti: (bi_, 0, 0)),
            pl.BlockSpec((LRU_CONV, w), lambda bi_, ti: (0, 0)),
            vec(),
            pl.BlockSpec((w, 2 * w), lambda bi_, ti: (0, 0)),
            vec(), vec(), vec(),
        ],
        out_specs=[
            pl.BlockSpec((1, r, w), lambda bi_, ti: (bi_, ti, 0)),
            pl.BlockSpec((1, stride, w), lambda bi_, ti: (bi_, 0, 0)),
            pl.BlockSpec((1, pad, w), lambda bi_, ti: (bi_, 0, 0)),
        ],
        out_shape=[
            jax.ShapeDtypeStruct((b, t, w), F32),
            jax.ShapeDtypeStruct((b, stride, w), F32),
            jax.ShapeDtypeStruct((b, pad, w), F32),
        ],
        scratch_shapes=[pltpu.VMEM((pad, w), F32), pltpu.VMEM((stride, w), F32)],
        compiler_params=_params("arbitrary", "arbitrary"),
        name="rglru",
    )(proj, proj, tail0, h0, cw, cb.reshape(1, w), wg_bf16, ba.reshape(1, w), bi.reshape(1, w),
      lam.reshape(1, w))
    return ob, h_last, tail


def _block_diag(wb):
    nb, c, _ = wb.shape
    eye = jnp.eye(nb, dtype=wb.dtype)
    return (eye[:, None, :, None] * wb[:, :, None, :]).reshape(nb * c, nb * c)


def _post_kernel(*refs, n_mix, stride, pad):
    h_ref = refs[0]
    a_refs = refs[1:1 + n_mix]
    wo_refs = refs[1 + n_mix:1 + 2 * n_mix]
    (p_ref, nf_ref, wg_ref, wu_ref, wout_ref, cw_ref, cb_ref, cst_ref, pgn_ref, wpg_ref, wpp_ref, ppn_ref,
     o_ref, fc_ref, h1_sc, xn_sc, acc_sc, carry_sc) = refs[1 + 2 * n_mix:]
    i = pl.program_id(1)
    j = pl.program_id(2)
    tm = h_ref.shape[0]

    @pl.when(j == 0)
    def _():
        y = h_ref[...]
        for a_ref, wo_ref in zip(a_refs, wo_refs):
            y = y + _dot(a_ref[...].astype(BF16), wo_ref[...])
        h1_sc[...] = y
        xn_sc[...] = _rms(y, nf_ref[...]).astype(BF16)
        acc_sc[...] = jnp.zeros(acc_sc.shape, F32)

    @pl.when(i == 0)
    def _():
        carry_sc[j] = cst_ref[0]

    xn = xn_sc[...]
    tf = wg_ref.shape[1]
    chunk = min(tf, MXU_WIDTH)
    contrib = None
    for c in range(tf // chunk):
        cols = slice(c * chunk, (c + 1) * chunk)
        g = _dot(xn, wg_ref[:, cols])
        u = _dot(xn, wu_ref[:, cols])
        xp = jnp.concatenate([carry_sc[j, :, cols], g], axis=0)
        gc = cb_ref[:, cols]
        for kk in range(FFN_CONV):
            off = pad - (FFN_CONV - 1 - kk) * stride
            gc = gc + xp[off:off + tm] * cw_ref[kk:kk + 1, cols]
        carry_sc[j, :, cols] = xp[tm:tm + pad]
        part = _dot((_gelu(gc) * u).astype(BF16), wout_ref[cols, :])
        contrib = part if contrib is None else contrib + part
    acc_sc[...] += contrib

    @pl.when(i == pl.num_programs(1) - 1)
    def _():
        fc_ref[0, j] = carry_sc[j]

    @pl.when(j == pl.num_programs(2) - 1)
    def _():
        h2 = h1_sc[...] + acc_sc[...]
        gate = jax.nn.sigmoid(_dot(_rms(h2, pgn_ref[...]).astype(BF16), wpg_ref[...]))
        e = _dot(p_ref[0].astype(BF16), wpp_ref[...])
        o_ref[...] = h2 + _rms(e, ppn_ref[...]) * gate


def _post_mixer(h, mix, wo_bf16, p, nf, wffn_in_bf16, wffn_out_bf16, cw, cb, cst, pgn, wpg_bf16, wpp_bf16,
                ppn, *, p_layer, n_seq, stride, tm, tf):
    n, d = h.shape
    dff = wffn_out_bf16.shape[0]
    nj = dff // tf
    pad = cst.shape[1]
    ni = n // n_seq // tm
    n_mix = len(mix)
    row = lambda width: pl.BlockSpec((tm, width), lambda b, i, j: (b * ni + i, 0))
    full = lambda shape: pl.BlockSpec(shape, lambda b, i, j: (0, 0))
    in_specs = (
        [row(d)] + [row(a.shape[1]) for a in mix] + [full(w_.shape) for w_ in wo_bf16]
        + [
            pl.BlockSpec((1, tm, p.shape[2]), lambda b, i, j: (p_layer, b * ni + i, 0)),
            full((1, d)),
            pl.BlockSpec((d, tf), lambda b, i, j: (0, j)),
            pl.BlockSpec((d, tf), lambda b, i, j: (0, nj + j)),
            pl.BlockSpec((tf, d), lambda b, i, j: (j, 0)),
            pl.BlockSpec((FFN_CONV, tf), lambda b, i, j: (0, j)),
            pl.BlockSpec((1, tf), lambda b, i, j: (0, j)),
            pl.BlockSpec((1, pad, tf), lambda b, i, j: (b, 0, j)),
            full((1, d)),
            full(wpg_bf16.shape),
            full(wpp_bf16.shape),
            full((1, d)),
        ])
    out, fc = pl.pallas_call(
        functools.partial(_post_kernel, n_mix=n_mix, stride=stride, pad=pad),
        grid=(n_seq, ni, nj),
        in_specs=in_specs,
        out_specs=[
            pl.BlockSpec((tm, d), lambda b, i, j: (b * ni + i, 0)),
            pl.BlockSpec((1, nj, pad, tf), lambda b, i, j: (b, 0, 0, 0)),
        ],
        out_shape=[
            jax.ShapeDtypeStruct((n, d), F32),
            jax.ShapeDtypeStruct((n_seq, nj, pad, tf), F32),
        ],
        scratch_shapes=[
            pltpu.VMEM((tm, d), F32),
            pltpu.VMEM((tm, d), BF16),
            pltpu.VMEM((tm, d), F32),
            pltpu.VMEM((nj, pad, tf), F32),
        ],
        compiler_params=_params("arbitrary", "arbitrary", "arbitrary"),
        name="post_mixer",
    )(h, *mix, *wo_bf16, p, nf.reshape(1, d), wffn_in_bf16, wffn_in_bf16, wffn_out_bf16, cw,
      cb.reshape(1, dff), cst, pgn.reshape(1, d), wpg_bf16, wpp_bf16, ppn.reshape(1, d))
    return out, jnp.transpose(fc, (0, 2, 1, 3)).reshape(n_seq, pad, dff)


def _t5_bucket(dist):
    n = jnp.maximum(dist, 0)
    max_exact = REL_BUCKETS // 2
    nf = jnp.maximum(n, max_exact).astype(F32)
    large = max_exact + (jnp.log(nf / max_exact) / math.log(REL_MAX_DIST / max_exact)
                         * (REL_BUCKETS - max_exact)).astype(jnp.int32)
    large = jnp.minimum(large, REL_BUCKETS - 1)
    return jnp.where(n < max_exact, n, large)


def _bias_table(rel_bias, dist):
    onehot = (_t5_bucket(jnp.asarray(dist))[..., None] == jnp.arange(REL_BUCKETS)).astype(F32)
    return jnp.einsum('...b,bh->h...', onehot, rel_bias.astype(F32), precision=lax.Precision.HIGHEST)


def _top3(sc, idxf, n, axis=-1):
    picks = []
    for _ in range(MOBA_TOPK):
        m = jnp.max(sc, axis=axis, keepdims=True)
        first = jnp.min(jnp.where(sc == m, idxf, float(n)), axis=axis, keepdims=True)
        picks.append((first, m))
        sc = jnp.where(idxf == first, -jnp.inf, sc)
    return picks


def _moba_prompt_kernel(q_ref, k_ref, vt_ref, bo_ref, bp_ref, bf_ref, o_ref, km_sc, s0_sc, s1_sc):
    i = pl.program_id(2)
    nb = km_sc.shape[0]
    blk = MOBA_BLOCK
    hd = C_HEAD_DIM
    ext = hd - MASK_ROWS

    @pl.when(i == 0)
    def _():
        def body(jb, c):
            kb = k_ref[0, pl.ds(pl.multiple_of(jb * blk, blk), blk), :]
            km_sc[pl.ds(jb, 1), :] = jnp.sum(kb, axis=0, keepdims=True) * (1.0 / blk)
            return c
        lax.fori_loop(0, nb, body, 0)

    q_t = q_ref[0].T
    km = km_sc[...]
    km_lo = lax.broadcasted_iota(jnp.int32, (nb, LANES), 1) < hd
    rowf = lax.broadcasted_iota(jnp.int32, (nb, blk), 0).astype(F32)
    i_f = i.astype(F32)

    def mask_rows(kmh):
        sc = jnp.dot(kmh, q_t, precision=lax.Precision.HIGHEST, preferred_element_type=F32)
        sc = jnp.where(rowf < i_f, sc, -jnp.inf)
        sel = jnp.zeros((nb, blk), F32)
        for first, m in _top3(sc, rowf, nb, axis=0):
            sel = jnp.where((rowf == first) & (m > -jnp.inf), 1.0, sel)
        mask = jnp.where(sel > 0.0, 0.0, NEG)
        if nb < MASK_ROWS:
            mask = jnp.concatenate([mask, jnp.zeros((MASK_ROWS - nb, blk), F32)], axis=0)
        return mask

    def bias_rows(bias):
        hi = bias.astype(BF16).astype(F32)
        r = lax.broadcasted_iota(jnp.int32, (ext, blk), 0)
        return jnp.where(r == 0, hi, jnp.where(r == 1, bias - hi, 0.0))

    c = hd ** -0.5 * LOG2E
    qp_a = jnp.concatenate([q_t[:hd] * c, mask_rows(jnp.where(km_lo, km, 0.0)), bias_rows(bf_ref[0])],
                           axis=0).astype(BF16)
    qp_b = jnp.concatenate([mask_rows(jnp.where(km_lo, 0.0, km)), bias_rows(bf_ref[1]), q_t[hd:] * c],
                           axis=0).astype(BF16)

    def logits(jb, n_blk, block_ids, far):
        start = pl.multiple_of(jb * blk, blk)
        lane = lax.broadcasted_iota(jnp.int32, (1, LANES), 1)
        lo = lax.broadcasted_iota(jnp.int32, (blk, LANES), 1) < hd
        kbs_a, kbs_b = [], []
        for g in range(n_blk):
            kb = k_ref[0, pl.ds(pl.multiple_of(start + g * blk, blk), blk), :]
            e_a = jnp.zeros((1, LANES), F32)
            e_b = jnp.zeros((1, LANES), F32)
            if block_ids is not None:
                e_a = jnp.where(lane == hd + block_ids[g], 1.0, e_a)
                e_b = jnp.where(lane == block_ids[g], 1.0, e_b)
            if far:
                e_a = jnp.where((lane == hd + MASK_ROWS) | (lane == hd + MASK_ROWS + 1), 1.0, e_a)
                e_b = jnp.where((lane == MASK_ROWS) | (lane == MASK_ROWS + 1), 1.0, e_b)
            kbs_a.append(jnp.where(lo, kb, e_a).astype(BF16))
            kbs_b.append(jnp.where(lo, e_b, kb).astype(BF16))
        kb_a = jnp.concatenate(kbs_a, axis=0) if n_blk > 1 else kbs_a[0]
        kb_b = jnp.concatenate(kbs_b, axis=0) if n_blk > 1 else kbs_b[0]
        return _dot(kb_a, qp_a), _dot(kb_b, qp_b)

    def fold(carry, s_a, s_b, jb, n_blk):
        m_a, acc_a, m_b, acc_b = carry
        keys = n_blk * blk
        vt = vt_ref[0, :, pl.ds(pl.multiple_of(jb * blk, blk), keys)]
        ones = jnp.ones((PV_ONES, keys), F32)
        vt_a = jnp.concatenate([vt[:hd], ones], axis=0).astype(BF16)
        vt_b = jnp.concatenate([vt[hd:], ones], axis=0).astype(BF16)
        mn_a = jnp.maximum(m_a, jnp.max(s_a, axis=0, keepdims=True))
        mn_b = jnp.maximum(m_b, jnp.max(s_b, axis=0, keepdims=True))
        p_a = jnp.exp2(s_a - mn_a).astype(BF16)
        p_b = jnp.exp2(s_b - mn_b).astype(BF16)
        acc_a = jnp.exp2(m_a - mn_a) * acc_a + _dot(vt_a, p_a)
        acc_b = jnp.exp2(m_b - mn_b) * acc_b + _dot(vt_b, p_b)
        return mn_a, acc_a, mn_b, acc_b

    m0 = jnp.full((1, blk), NEG, F32)
    acc0 = jnp.zeros((hd + PV_ONES, blk), F32)
    carry = (m0, acc0, m0, acc0)
    no_mask = -LANES
    first = i == 0
    near = jnp.maximum(i - 1, 0)
    s_a, s_b = logits(near, GROUP, [jnp.where(first, no_mask, i - 1), jnp.where(first, nb - 1, no_mask)], False)
    s0_sc[0] = s_a + jnp.concatenate([jnp.where(first, bo_ref[0], bp_ref[0]), bo_ref[0]], axis=0)
    s0_sc[1] = s_b + jnp.concatenate([jnp.where(first, bo_ref[1], bp_ref[1]), bo_ref[1]], axis=0)

    n_far = jnp.maximum(i - 1, 0)
    n_grp = (n_far + GROUP - 1) // GROUP

    def far_logits(grp, s_sc):
        jb = jnp.minimum(grp * GROUP, nb - GROUP)
        ids = [jnp.where(grp * GROUP + g < n_far, grp * GROUP + g, nb - 1) for g in range(GROUP)]
        s_sc[0], s_sc[1] = logits(jb, GROUP, ids, True)

    def far_fold(cy, grp, s_sc):
        return fold(cy, s_sc[0], s_sc[1], jnp.minimum(grp * GROUP, nb - GROUP), GROUP)

    far_logits(0, s1_sc)
    carry = fold(carry, s0_sc[0], s0_sc[1], near, GROUP)

    def far_pair(it, cy):
        far_logits(2 * it + 1, s0_sc)
        cy = far_fold(cy, 2 * it, s1_sc)
        far_logits(2 * it + 2, s1_sc)
        return far_fold(cy, 2 * it + 1, s0_sc)

    m_a, acc_a, m_b, acc_b = lax.fori_loop(0, (n_grp + 1) // 2, far_pair, carry)
    out_t = jnp.concatenate([acc_a[:hd] / acc_a[hd:hd + 1], acc_b[:hd] / acc_b[hd:hd + 1]], axis=0)
    o_ref[0] = out_t.T


def _moba_prompt(q, k, v_t, rel_bias):
    b, t, width = q.shape
    blk = MOBA_BLOCK
    nb = t // blk
    assert nb <= MASK_ROWS
    hp = width // LANES
    ll = np.arange(blk)[:, None]
    ii = np.arange(blk)[None, :]
    bias_own = jnp.where(ii >= ll, _bias_table(rel_bias, np.maximum(ii - ll, 0)) * LOG2E, NEG)
    bias_prev = _bias_table(rel_bias, blk + ii - ll) * LOG2E
    bias_far = (_bias_table(rel_bias, np.array([2 * blk])) * LOG2E).reshape(-1, 1, 1)
    pair = lambda shape: pl.BlockSpec((2,) + shape, lambda bi, hi, qi: (hi, 0, 0))
    return pl.pallas_call(
        _moba_prompt_kernel,
        grid=(b, hp, nb),
        in_specs=[
            pl.BlockSpec((1, blk, LANES), lambda bi, hi, qi: (bi, qi, hi)),
            pl.BlockSpec((1, t, LANES), lambda bi, hi, qi: (bi, 0, hi)),
            pl.BlockSpec((1, LANES, t), lambda bi, hi, qi: (bi, hi, 0)),
            pair((blk, blk)), pair((blk, blk)), pair((1, 1)),
        ],
        out_specs=pl.BlockSpec((1, blk, LANES), lambda bi, hi, qi: (bi, qi, hi)),
        out_shape=jax.ShapeDtypeStruct((b, t, width), F32),
        scratch_shapes=[pltpu.VMEM((nb, LANES), F32),
                        pltpu.VMEM((2, GROUP * blk, blk), F32),
                        pltpu.VMEM((2, GROUP * blk, blk), F32)],
        compiler_params=_params("arbitrary", "arbitrary", "arbitrary"),
        name="moba_prompt",
    )(q, k, v_t, bias_own, bias_prev, bias_far)


def _kmean_kernel(pt_ref, *refs):
    page_refs, o_ref = refs[:-1], refs[-1]
    j = pl.program_id(1)

    @pl.when(j == 0)
    def _():
        o_ref[...] = jnp.zeros(o_ref.shape, F32)

    lane = lax.broadcasted_iota(jnp.int32, o_ref.shape[1:], 2)
    out = o_ref[0]
    for g in range(len(page_refs) // PAGES_PER_BLOCK):
        pages = page_refs[g * PAGES_PER_BLOCK:(g + 1) * PAGES_PER_BLOCK]
        tot = pages[0][0]
        for pr in pages[1:]:
            tot = tot + pr[0]
        col = jnp.sum(tot, axis=-1, keepdims=True) * (1.0 / MOBA_BLOCK)
        out = jnp.where(lane == j * KMEAN_BLOCKS_PER_STEP + g, col, out)
    o_ref[0] = out


def _block_means(ck_t, page_table, nb):
    _, h, dh, page = ck_t.shape
    b = page_table.shape[0]
    g = KMEAN_BLOCKS_PER_STEP
    assert nb % g == 0
    n_pages = g * PAGES_PER_BLOCK
    page_spec = lambda k: pl.BlockSpec((1, h, dh, page), lambda bi, j, pt: (pt[bi, n_pages * j + k], 0, 0, 0))
    return pl.pallas_call(
        _kmean_kernel,
        grid_spec=pltpu.PrefetchScalarGridSpec(
            num_scalar_prefetch=1,
            grid=(b, nb // g),
            in_specs=[page_spec(k) for k in range(n_pages)],
            out_specs=pl.BlockSpec((1, h, dh, nb), lambda bi, j, pt: (bi, 0, 0, 0)),
        ),
        out_shape=jax.ShapeDtypeStruct((b, h, dh, nb), F32),
        compiler_params=_params("arbitrary", "arbitrary"),
        name="moba_block_means",
    )(page_table, *([ck_t] * n_pages))


def _select_kernel(qt_ref, km_ref, o_ref):
    s_len, _, h = qt_ref.shape[1:]
    nb = km_ref.shape[3]
    rows = s_len * h
    rid = lax.broadcasted_iota(jnp.int32, (rows, nb), 0)
    sc = jnp.zeros((rows, nb), F32)
    for s in range(s_len):
        qt = qt_ref[0, s]
        for hh in range(h):
            sc = jnp.where(rid == s * h + hh, jnp.sum(km_ref[0, hh] * qt[:, hh:hh + 1], axis=0, keepdims=True), sc)
    colf = lax.broadcasted_iota(jnp.int32, (rows, nb), 1).astype(F32)
    picks = [jnp.minimum(first, nb - 1.0).astype(jnp.int32) for first, _ in _top3(sc, colf, nb)]
    lane = lax.broadcasted_iota(jnp.int32, (rows, LANES), 1)
    o_ref[0] = jnp.where(lane == 0, picks[0], jnp.where(lane == 1, picks[1], picks[2]))


def _select_blocks(q_t, kmean_t):
    b, s_len, dh, h = q_t.shape
    nb = kmean_t.shape[3]
    out = pl.pallas_call(
        _select_kernel,
        grid=(b,),
        in_specs=[
            pl.BlockSpec((1, s_len, dh, h), lambda bi: (bi, 0, 0, 0)),
            pl.BlockSpec((1, h, dh, nb), lambda bi: (bi, 0, 0, 0)),
        ],
        out_specs=pl.BlockSpec((1, s_len * h, LANES), lambda bi: (bi, 0, 0)),
        out_shape=jax.ShapeDtypeStruct((b, s_len * h, LANES), jnp.int32),
        compiler_params=_params("arbitrary"),
        name="moba_select",
    )(q_t, kmean_t)
    return out[:, :, :MOBA_TOPK].reshape(b, s_len, h, MOBA_TOPK)


def _moba_sample_kernel(pt_ref, sel_ref, ck_ref, cv_ref, qt_ref, kn_ref, vn_ref, bl_ref, bo_ref, bf_ref,
                        o_ref, kbuf, vbuf, sem, *, nb):
    b = pl.program_id(0)
    s = pl.program_id(1)
    s_len = pl.num_programs(1)
    _, h, _, dh, _ = kbuf.shape
    ppb = PAGES_PER_BLOCK
    step = b * s_len + s
    cur = step % 2

    def copies(st, buf):
        out = []
        for hh in range(h):
            for n in range(MOBA_TOPK):
                idx = sel_ref[(st * h + hh) * MOBA_TOPK + n]
                for pg in range(ppb):
                    phys = pt_ref[st // s_len, idx * ppb + pg]
                    slot = n * ppb + pg
                    out.append(pltpu.make_async_copy(ck_ref.at[phys, hh], kbuf.at[buf, hh, slot], sem.at[buf, 0]))
                    out.append(pltpu.make_async_copy(cv_ref.at[phys, hh], vbuf.at[buf, hh, slot], sem.at[buf, 1]))
        return out

    @pl.when(step == 0)
    def _():
        for cp in copies(step, cur):
            cp.start()

    @pl.when(step + 1 < pl.num_programs(0) * s_len)
    def _():
        for cp in copies(step + 1, 1 - cur):
            cp.start()

    for cp in copies(step, cur):
        cp.wait()

    kbuf = kbuf.at[cur]
    vbuf = vbuf.at[cur]
    lane = lax.broadcasted_iota(jnp.int32, (dh, LANES), 1)
    qt = qt_ref[0, 0] * (dh ** -0.5)
    out = jnp.zeros((dh, LANES), F32)
    for hh in range(h):
        qc = qt[:, hh:hh + 1]
        far = bf_ref[hh:hh + 1, :]
        logits = []
        for n in range(MOBA_TOPK):
            idx = sel_ref[((b * s_len + s) * h + hh) * MOBA_TOPK + n]
            for pg in range(ppb):
                lg = jnp.sum(kbuf[hh, n * ppb + pg] * qc, axis=0, keepdims=True)
                logits.append(lg + jnp.where(idx == nb - 1, bl_ref[0, hh, pg:pg + 1, :], far))
        lown = jnp.sum(kn_ref[0, hh] * qc, axis=0, keepdims=True) + bo_ref[0, hh:hh + 1, :]
        m = jnp.max(lown, axis=-1, keepdims=True)
        for lg in logits:
            m = jnp.maximum(m, jnp.max(lg, axis=-1, keepdims=True))
        pown = jnp.exp(lown - m)
        den = jnp.sum(pown, axis=-1, keepdims=True)
        col = jnp.sum(vn_ref[0, hh] * pown, axis=-1, keepdims=True)
        for slot, lg in enumerate(logits):
            p = jnp.exp(lg - m)
            den = den + jnp.sum(p, axis=-1, keepdims=True)
            col = col + jnp.sum(vbuf[hh, slot] * p, axis=-1, keepdims=True)
        out = jnp.where(lane == hh, col / den, out)
    o_ref[0, 0] = out


def _moba_sample(q, k_new, v_new, ck_t, cv_t, page_table, rel_bias, past_len):
    b, s_len, h, dh = q.shape
    page = ck_t.shape[3]
    blk = MOBA_BLOCK
    assert past_len % blk == 0 and blk == PAGES_PER_BLOCK * page
    nb = past_len // blk
    assert nb >= MOBA_TOPK
    q_t = jnp.transpose(q, (0, 1, 3, 2))
    kmean_t = _block_means(ck_t, page_table, nb)
    sel = _select_blocks(q_t, kmean_t)

    ss = np.arange(s_len)
    bias_last = _bias_table(rel_bias, blk + ss[:, None] - np.arange(blk)[None, :])
    bias_last = jnp.transpose(bias_last, (1, 0, 2)).reshape(s_len, h, PAGES_PER_BLOCK, page)
    own_d = ss[:, None] - ss[None, :]
    bias_own = jnp.where(own_d >= 0, _bias_table(rel_bias, np.maximum(own_d, 0)), NEG)
    bias_own = jnp.transpose(bias_own, (1, 0, 2))
    bias_far = _bias_table(rel_bias, np.array([2 * blk]))
    kn_t = jnp.transpose(k_new, (0, 2, 3, 1))
    vn_t = jnp.transpose(v_new, (0, 2, 3, 1))

    out = pl.pallas_call(
        functools.partial(_moba_sample_kernel, nb=nb),
        grid_spec=pltpu.PrefetchScalarGridSpec(
            num_scalar_prefetch=2,
            grid=(b, s_len),
            in_specs=[
                pl.BlockSpec(memory_space=pl.ANY),
                pl.BlockSpec(memory_space=pl.ANY),
                pl.BlockSpec((1, 1, dh, h), lambda bi, si, pt, sl: (bi, si, 0, 0)),
                pl.BlockSpec((1, h, dh, s_len), lambda bi, si, pt, sl: (bi, 0, 0, 0)),
                pl.BlockSpec((1, h, dh, s_len), lambda bi, si, pt, sl: (bi, 0, 0, 0)),
                pl.BlockSpec((1, h, PAGES_PER_BLOCK, page), lambda bi, si, pt, sl: (si, 0, 0, 0)),
                pl.BlockSpec((1, h, s_len), lambda bi, si, pt, sl: (si, 0, 0)),
                pl.BlockSpec((h, 1), lambda bi, si, pt, sl: (0, 0)),
            ],
            out_specs=pl.BlockSpec((1, 1, dh, LANES), lambda bi, si, pt, sl: (bi, si, 0, 0)),
            scratch_shapes=[
                pltpu.VMEM((2, h, PAGES_PER_BLOCK * MOBA_TOPK, dh, page), F32),
                pltpu.VMEM((2, h, PAGES_PER_BLOCK * MOBA_TOPK, dh, page), F32),
                pltpu.SemaphoreType.DMA((2, 2)),
            ],
        ),
        out_shape=jax.ShapeDtypeStruct((b, s_len, dh, LANES), F32),
        compiler_params=_params("arbitrary", "arbitrary"),
        name="moba_sample",
    )(page_table, sel.reshape(-1), ck_t, cv_t, q_t, kn_t, vn_t, bias_last, bias_own, bias_far)
    return jnp.transpose(out[:, :, :, :h], (0, 1, 3, 2))


def _prep_weights(w):
    bf = lambda a: a.astype(BF16)
    n_even = w['w0_in'].shape[0]
    n_odd = w['c_in'].shape[0]
    out = dict(w)
    out['w0_in'] = bf(w['w0_in'])
    out['w0_out'] = bf(w['w0_out'])
    out['lru_wg'] = [bf(jnp.concatenate([_block_diag(w['lru_wa'][e]), _block_diag(w['lru_wi'][e])], axis=1))
                     for e in range(n_even)]
    out['c_in'] = bf(w['c_in'])
    out['c_out'] = bf(w['c_out'])
    out['c_gain'] = [jnp.concatenate([jnp.tile(w['c_qn'][o], C_HEADS), jnp.tile(w['c_kn'][o], C_HEADS),
                                      jnp.ones((C_HEADS * C_HEAD_DIM,), F32)]).reshape(1, -1)
                     for o in range(n_odd)]
    out['ffn_in'] = bf(w['ffn_in'])
    out['ffn_out'] = bf(w['ffn_out'])
    out['ple_proj'] = bf(w['ple_proj'])
    out['ple_gate'] = bf(w['ple_gate'])
    return out


def _post(w, l, h, mix, wo, p, p_layer, cst, *, n_seq, stride, tm, tf):
    return _post_mixer(h, mix, wo, p, w['norm_ffn'][l], w['ffn_in'][l], w['ffn_out'][l], w['ffn_conv_w'][l],
                       w['ffn_conv_b'][l], cst, w['ple_gate_norm'][l], w['ple_gate'][l], w['ple_proj'][l],
                       w['ple_post_norm'][l], p_layer=p_layer, n_seq=n_seq, stride=stride, tm=tm, tf=tf)


def _prompt_trunk(x, p, w, rel_bias):
    b, t, d = x.shape
    n = b * t
    depth = p.shape[0]
    h = x.reshape(n, d)
    tm = 512
    rets, lrus, lconvs, fconvs, ks, vs = [], [], [], [], [], []
    lw = w['lru_conv_w'].shape[2]
    dff = w['ffn_out'].shape[1]
    for l in range(depth):
        cst = jnp.zeros((b, SUBLANES, dff), F32)
        if l % 2 == 0:
            e = l // 2
            proj = _norm_proj(h, w['norm_mix'][l], w['w0_in'][e], tm=512, tn=1024).reshape(b, t, -1)
            o, s_new = _retention(proj, jnp.zeros((b, RET_HEADS, RET_DIM, RET_DIM), F32), jnp.arange(t),
                                  w['ret_gn_g'][e], w['ret_gn_b'][e], c_eff=RET_CHUNK, c_pad=RET_CHUNK,
                                  rows_per_step=512)
            ob, h_last, tail = _lru(proj, jnp.zeros((b, SUBLANES, lw), F32), jnp.zeros((b, 1, lw), F32),
                                    w['lru_conv_w'][e], w['lru_conv_b'][e], w['lru_wg'][e], w['lru_ba'][e],
                                    w['lru_bi'][e], w['lru_lam'][e], stride=1, rows_per_step=512)
            rets.append(s_new)
            lrus.append(h_last[:, 0])
            lconvs.append(tail[:, SUBLANES - (LRU_CONV - 1):])
            rw = o.shape[-1]
            mix = [o.reshape(n, rw), ob.reshape(n, lw)]
            wo = [w['w0_out'][e][:rw], w['w0_out'][e][rw:]]
        else:
            o_ = l // 2
            q, k, k_t, v_t = _norm_proj(h, w['norm_mix'][l], w['c_in'][o_], tm=512, tn=1024,
                                        head_gain=w['c_gain'][o_], n_rms_tiles=2,
                                        plan=((False,), (False, True), (True,)), n_seq=b)
            cw = q.shape[1]
            att = _moba_prompt(q.reshape(b, t, cw), k.reshape(b, t, cw), v_t, rel_bias)
            heads = lambda a: jnp.transpose(a.reshape(b, C_HEADS, C_HEAD_DIM, t), (0, 3, 1, 2))
            ks.append(heads(k_t))
            vs.append(heads(v_t))
            mix = [att.reshape(n, cw)]
            wo = [w['c_out'][o_]]
        h, fc = _post(w, l, h, mix, wo, p.reshape(depth, n, -1), l, cst, n_seq=b, stride=1, tm=tm, tf=512)
        fconvs.append(fc[:, SUBLANES - (FFN_CONV - 1):])
    return (h.reshape(b, t, d), jnp.stack(rets), jnp.stack(lrus), jnp.stack(lconvs), jnp.stack(fconvs),
            jnp.stack(ks), jnp.stack(vs))


def _sample_trunk(x, p, ret0, lru0, lconv0, fconv0, cache_k, cache_v, page_table, w, rel_bias):
    b, t, d = x.shape
    n = b * t
    depth = p.shape[0]
    past_len = page_table.shape[1] * cache_k.shape[2]
    tmaj = lambda a: jnp.transpose(a, (1, 0) + tuple(range(2, a.ndim)))
    h = tmaj(x).reshape(n, d)
    rets, lrus, lconvs, fconvs, ks, vs = [], [], [], [], [], []
    lw = w['lru_conv_w'].shape[2]
    for l in range(depth):
        cst = tmaj(fconv0[l]).reshape(1, (FFN_CONV - 1) * b, -1)
        if l % 2 == 0:
            e = l // 2
            proj = _norm_proj(h, w['norm_mix'][l], w['w0_in'][e], tm=n, tn=1024)
            rcols = 4 * RET_HEADS * RET_DIM
            pr = tmaj(proj[:, :rcols].reshape(t, b, rcols))
            pr = jnp.pad(pr, ((0, 0), (0, SUBLANES - t), (0, 0)))
            pos = past_len + jnp.arange(SUBLANES)
            o, s_new = _retention(pr, ret0[e].astype(F32), pos, w['ret_gn_g'][e], w['ret_gn_b'][e],
                                  c_eff=t, c_pad=SUBLANES, rows_per_step=SUBLANES)
            o = tmaj(o[:, :t]).reshape(n, -1)
            tail0 = tmaj(lconv0[e]).reshape(1, (LRU_CONV - 1) * b, lw)
            ob, h_last, tail = _lru(proj.reshape(1, n, -1), tail0, lru0[e].reshape(1, b, lw),
                                    w['lru_conv_w'][e], w['lru_conv_b'][e], w['lru_wg'][e], w['lru_ba'][e],
                                    w['lru_bi'][e], w['lru_lam'][e], stride=b, rows_per_step=n)
            rets.append(s_new)
            lrus.append(h_last[0])
            lconvs.append(tmaj(tail.reshape(LRU_CONV - 1, b, lw)))
            rw = o.shape[-1]
            mix = [o, ob.reshape(n, lw)]
            wo = [w['w0_out'][e][:rw], w['w0_out'][e][rw:]]
        else:
            o_ = l // 2
            q, k, v = _norm_proj(h, w['norm_mix'][l], w['c_in'][o_], tm=n, tn=1024, head_gain=w['c_gain'][o_],
                                 n_rms_tiles=2, plan=((False,), (False,), (False,)))
            heads = lambda a: tmaj(a.reshape(t, b, C_HEADS, C_HEAD_DIM))
            q, k, v = heads(q), heads(k), heads(v)
            ck_t = jnp.transpose(cache_k[o_], (0, 2, 3, 1))
            cv_t = jnp.transpose(cache_v[o_], (0, 2, 3, 1))
            att = _moba_sample(q, k, v, ck_t, cv_t, page_table, rel_bias, past_len)
            ks.append(k)
            vs.append(v)
            mix = [tmaj(att).reshape(n, -1)]
            wo = [w['c_out'][o_]]
        h, fc = _post(w, l, h, mix, wo, tmaj(p[l]).reshape(1, n, -1), 0, cst, n_seq=1, stride=b, tm=n, tf=512)
        fconvs.append(tmaj(fc.reshape(FFN_CONV - 1, b, -1)))
    return (tmaj(h.reshape(t, b, d)), jnp.stack(rets), jnp.stack(lrus), jnp.stack(lconvs), jnp.stack(fconvs),
            jnp.stack(ks), jnp.stack(vs))


def kernel(x_prompt, x_sample, state_ret, state_lru, state_lru_conv, state_ffn_conv, cache_k, cache_v, page_table, p_prompt, p_sample, norm_mix, norm_ffn, w0_in, ret_gn_g, ret_gn_b, lru_conv_w, lru_conv_b, lru_wa, lru_ba, lru_wi, lru_bi, lru_lam, w0_out, c_in, c_qn, c_kn, c_out, rel_bias, ffn_in, ffn_conv_w, ffn_conv_b, ffn_out, ple_proj, ple_gate_norm, ple_gate, ple_post_norm):
    w = _prep_weights(dict(
        norm_mix=norm_mix, norm_ffn=norm_ffn, w0_in=w0_in, ret_gn_g=ret_gn_g, ret_gn_b=ret_gn_b,
        lru_conv_w=lru_conv_w, lru_conv_b=lru_conv_b, lru_wa=lru_wa, lru_ba=lru_ba, lru_wi=lru_wi,
        lru_bi=lru_bi, lru_lam=lru_lam, w0_out=w0_out, c_in=c_in, c_qn=c_qn, c_kn=c_kn, c_out=c_out,
        ffn_in=ffn_in, ffn_conv_w=ffn_conv_w, ffn_conv_b=ffn_conv_b, ffn_out=ffn_out, ple_proj=ple_proj,
        ple_gate_norm=ple_gate_norm, ple_gate=ple_gate, ple_post_norm=ple_post_norm))
    y_p, ret_p, lru_p, lconv_p, fconv_p, k_p, v_p = _prompt_trunk(x_prompt, p_prompt, w, rel_bias)
    y_s, ret_s, lru_s, lconv_s, fconv_s, k_s, v_s = _sample_trunk(
        x_sample, p_sample, state_ret, state_lru, state_lru_conv, state_ffn_conv, cache_k, cache_v, page_table,
        w, rel_bias)
    return (y_p, y_s, ret_p, ret_s, lru_p, lru_s, lconv_p, lconv_s, fconv_p, fconv_s, k_p, k_s, v_p, v_s)
```

```python
import functools
import math

import numpy as np
import jax
import jax.numpy as jnp
from jax import lax
from jax.experimental import pallas as pl
from jax.experimental.pallas import tpu as pltpu

F32 = jnp.float32
BF16 = jnp.bfloat16
EPS = 1e-6
NEG = -1e30

LANES = 128
SUBLANES = 8
MXU_WIDTH = 256
VMEM_LIMIT_BYTES = 48 * 1024 * 1024

RET_HEADS = 4
RET_DIM = 128
RET_CHUNK = 128
ROPE_THETA = 10000.0
LRU_C = 8.0
LRU_CONV = 4
FFN_CONV = 3
C_HEADS = 16
C_HEAD_DIM = 64
MOBA_BLOCK = 256
MOBA_TOPK = 3
REL_BUCKETS = 32
REL_MAX_DIST = 128
LOG2E = 1.4426950408889634
MASK_ROWS = 32
PV_ONES = 16
GROUP = 2
KMEAN_BLOCKS_PER_STEP = 4
PAGES_PER_BLOCK = 2


def _params(*sem):
    return pltpu.CompilerParams(dimension_semantics=sem, vmem_limit_bytes=VMEM_LIMIT_BYTES)


def _rms(x, g):
    return x * lax.rsqrt(jnp.mean(x * x, axis=-1, keepdims=True) + EPS) * g


def _gelu(x):
    return 0.5 * x * (1.0 + jnp.tanh(0.7978845608028654 * (x + 0.044715 * (x * x * x))))


def _dot(a, b):
    return jnp.dot(a, b, preferred_element_type=F32)


def _dot_nt(a, b):
    return lax.dot_general(a, b, (((1,), (1,)), ((), ())), preferred_element_type=F32)


def _dot_tn(a, b):
    return lax.dot_general(a, b, (((0,), (0,)), ((), ())), preferred_element_type=F32)


def _head_rms(y, hd):
    tm, tn = y.shape
    lo = lax.broadcasted_iota(jnp.int32, (tm, LANES), 1) < hd
    outs = []
    for s in range(tn // LANES):
        ys = y[:, s * LANES:(s + 1) * LANES]
        y2 = ys * ys
        s0 = jnp.sum(jnp.where(lo, y2, 0.0), axis=-1, keepdims=True)
        s1 = jnp.sum(jnp.where(lo, 0.0, y2), axis=-1, keepdims=True)
        r = lax.rsqrt(jnp.where(lo, s0, s1) * (1.0 / hd) + EPS)
        outs.append(ys * r)
    return jnp.concatenate(outs, axis=-1)


def _norm_proj_kernel(x_ref, g_ref, w_ref, hg_ref, *o_refs, n_rms_tiles, plan, tn):
    nout = w_ref.shape[1]
    xn = _rms(x_ref[...], g_ref[...]).astype(BF16)
    k = 0
    for t in range(nout // tn):
        cols = slice(t * tn, (t + 1) * tn)
        y = _dot(xn, w_ref[:, cols])
        if not plan:
            o_refs[0][:, cols] = y
            continue
        if t < n_rms_tiles:
            y = _head_rms(y, C_HEAD_DIM) * hg_ref[:, cols]
        for transposed in plan[t]:
            if transposed:
                o_refs[k][0] = y.T
            else:
                o_refs[k][...] = y
            k += 1


def _norm_proj(x, g, w_bf16, *, tm, tn, head_gain=None, n_rms_tiles=0, plan=None, n_seq=1):
    n, d = x.shape
    nout = w_bf16.shape[1]
    nj = nout // tn
    if head_gain is None:
        head_gain = jnp.ones((1, nout), F32)
    if plan is None:
        out_shape = [jax.ShapeDtypeStruct((n, nout), F32)]
        out_specs = [pl.BlockSpec((tm, nout), lambda i: (i, 0))]
    else:
        assert len(plan) == nj
        t_len = n // n_seq
        nt = t_len // tm
        out_shape, out_specs = [], []
        for outs in plan:
            for transposed in outs:
                if transposed:
                    out_shape.append(jax.ShapeDtypeStruct((n_seq, tn, t_len), F32))
                    out_specs.append(pl.BlockSpec((1, tn, tm), lambda i: (i // nt, 0, i % nt)))
                else:
                    out_shape.append(jax.ShapeDtypeStruct((n, tn), F32))
                    out_specs.append(pl.BlockSpec((tm, tn), lambda i: (i, 0)))
    outs = pl.pallas_call(
        functools.partial(_norm_proj_kernel, n_rms_tiles=n_rms_tiles, plan=plan, tn=tn),
        grid=(n // tm,),
        in_specs=[
            pl.BlockSpec((tm, d), lambda i: (i, 0)),
            pl.BlockSpec((1, d), lambda i: (0, 0)),
            pl.BlockSpec((d, nout), lambda i: (0, 0)),
            pl.BlockSpec((1, nout), lambda i: (0, 0)),
        ],
        out_specs=out_specs,
        out_shape=out_shape,
        compiler_params=_params("arbitrary"),
        name="norm_proj",
    )(x, g.reshape(1, d), w_bf16, head_gain)
    return outs[0] if plan is None else outs


def _ret_kernel(q_ref, k_ref, v_ref, g_ref, s0_ref, cos_ref, sin_ref, dm_ref, qd_ref, kd_ref, cd_ref,
                gg_ref, gb_ref, o_ref, s_ref, st_ref, *, c, nsub):
    t = pl.program_id(2)

    @pl.when(t == 0)
    def _():
        st_ref[...] = s0_ref[0, 0]

    dm = dm_ref[0]
    qd = qd_ref[0]
    kd = kd_ref[0]
    cd = cd_ref[0]
    gg = gg_ref[...]
    gb = gb_ref[...]
    for u in range(nsub):
        rows = slice(u * c, (u + 1) * c)
        cos = cos_ref[rows, :]
        sin = sin_ref[rows, :]
        q = q_ref[0, rows, :]
        k = k_ref[0, rows, :]
        q = q * cos + pltpu.roll(q, RET_DIM // 2, 1) * sin
        k = (k * cos + pltpu.roll(k, RET_DIM // 2, 1) * sin) * (RET_DIM ** -0.5)
        s = st_ref[...]
        qb = q.astype(BF16)
        vb = v_ref[0, rows, :].astype(BF16)
        att = _dot_nt(qb, k.astype(BF16)) * dm
        o = _dot(att.astype(BF16), vb) + _dot(qb, s.astype(BF16)) * qd
        st_ref[...] = s * cd + _dot_tn((k * kd).astype(BF16), vb)
        mu = jnp.mean(o, axis=-1, keepdims=True)
        oc = o - mu
        var = jnp.mean(oc * oc, axis=-1, keepdims=True)
        o = oc * lax.rsqrt(var + EPS) * gg + gb
        gate = g_ref[0, rows, :]
        o_ref[0, rows, :] = o * (gate * jax.nn.sigmoid(gate))

    @pl.when(t == pl.num_programs(2) - 1)
    def _():
        s_ref[0, 0] = st_ref[...]


def _retention(proj, s0, pos, gn_g, gn_b, *, c_eff, c_pad, rows_per_step):
    b, t, _ = proj.shape
    h = RET_HEADS
    half = RET_DIM // 2
    inv = ROPE_THETA ** (-jnp.arange(half, dtype=F32) / half)
    ang = pos.astype(F32)[:, None] * inv[None, :]
    cos2 = jnp.concatenate([jnp.cos(ang), jnp.cos(ang)], axis=-1)
    sin2 = jnp.concatenate([-jnp.sin(ang), jnp.sin(ang)], axis=-1)
    log_g = jnp.log1p(-jnp.exp2(-5.0 - jnp.arange(h, dtype=F32)))
    idx = jnp.arange(c_pad, dtype=F32)
    diff = idx[:, None] - idx[None, :]
    dmask = jnp.where(diff >= 0, jnp.exp(log_g[:, None, None] * jnp.maximum(diff, 0.0)), 0.0)
    q_dec = jnp.exp(log_g[:, None, None] * (idx[None, :, None] + 1.0))
    k_dec = jnp.exp(log_g[:, None, None] * (c_eff - 1.0 - idx[None, :, None]))
    c_dec = jnp.exp(log_g * c_eff).reshape(h, 1, 1)
    r = rows_per_step
    nsub = r // c_pad
    col = lambda off: pl.BlockSpec((1, r, RET_DIM), lambda bi, hi, ti, off=off: (bi, ti, off + hi))
    per_head = lambda shape: pl.BlockSpec((1,) + shape, lambda bi, hi, ti: (hi, 0, 0))
    o, s_new = pl.pallas_call(
        functools.partial(_ret_kernel, c=c_pad, nsub=nsub),
        grid=(b, h, t // r),
        in_specs=[
            col(0), col(h), col(2 * h), col(3 * h),
            pl.BlockSpec((1, 1, RET_DIM, RET_DIM), lambda bi, hi, ti: (bi, hi, 0, 0)),
            pl.BlockSpec((r, RET_DIM), lambda bi, hi, ti: (ti, 0)),
            pl.BlockSpec((r, RET_DIM), lambda bi, hi, ti: (ti, 0)),
            per_head((c_pad, c_pad)), per_head((c_pad, 1)), per_head((c_pad, 1)), per_head((1, 1)),
            pl.BlockSpec((1, RET_DIM), lambda bi, hi, ti: (0, hi)),
            pl.BlockSpec((1, RET_DIM), lambda bi, hi, ti: (0, hi)),
        ],
        out_specs=[
            pl.BlockSpec((1, r, RET_DIM), lambda bi, hi, ti: (bi, ti, hi)),
            pl.BlockSpec((1, 1, RET_DIM, RET_DIM), lambda bi, hi, ti: (bi, hi, 0, 0)),
        ],
        out_shape=[
            jax.ShapeDtypeStruct((b, t, h * RET_DIM), F32),
            jax.ShapeDtypeStruct((b, h, RET_DIM, RET_DIM), F32),
        ],
        scratch_shapes=[pltpu.VMEM((RET_DIM, RET_DIM), F32)],
        compiler_params=_params("arbitrary", "arbitrary", "arbitrary"),
        name="retention",
    )(proj, proj, proj, proj, s0, cos2, sin2, dmask, q_dec, k_dec, c_dec,
      gn_g.reshape(1, -1), gn_b.reshape(1, -1))
    return o, s_new


def _shift_rows(x, shift, fill):
    rows = lax.broadcasted_iota(jnp.int32, x.shape, 0)
    return jnp.where(rows >= shift, pltpu.roll(x, shift, 0), fill)


def _lru_kernel(bx_ref, bg_ref, tail0_ref, h0_ref, cw_ref, cb_ref, wg_ref, ba_ref, bi_ref, lam_ref,
                ob_ref, hl_ref, tail_ref, tail_sc, h_sc, *, stride, pad):
    t = pl.program_id(1)
    r, w = bx_ref.shape[1], bx_ref.shape[2]

    @pl.when(t == 0)
    def _():
        tail_sc[...] = tail0_ref[0]
        h_sc[...] = h0_ref[0]

    bx = bx_ref[0]
    xp = jnp.concatenate([tail_sc[...], bx], axis=0)
    xc = cb_ref[...]
    for j in range(LRU_CONV):
        off = pad - (LRU_CONV - 1 - j) * stride
        xc = xc + xp[off:off + r] * cw_ref[j:j + 1, :]
    new_tail = xp[r:r + pad]
    tail_sc[...] = new_tail
    tail_ref[0] = new_tail

    z = _dot(xc.astype(BF16), wg_ref[...])
    rg = jax.nn.sigmoid(z[:, :w] + ba_ref[...])
    ig = jax.nn.sigmoid(z[:, w:] + bi_ref[...])
    nl = -lam_ref[...]
    softplus = jnp.maximum(nl, 0.0) + jnp.log1p(jnp.exp(-jnp.abs(nl)))
    log_a = -LRU_C * rg * softplus
    a = jnp.exp(log_a)
    th = jnp.tanh(log_a)
    u = jnp.sqrt(-2.0 * th / (1.0 - th)) * (ig * xc)

    d = 1
    while d < r // stride:
        u = u + a * _shift_rows(u, d * stride, 0.0)
        a = a * _shift_rows(a, d * stride, 1.0)
        d *= 2
    hc = h_sc[...]
    if r > stride:
        hc = jnp.tile(hc, (r // stride, 1)) if stride > 1 else jnp.broadcast_to(hc, (r, w))
    hs = u + a * hc
    h_last = hs[r - stride:, :]
    h_sc[...] = h_last
    hl_ref[0] = h_last
    ob_ref[0] = hs * _gelu(bg_ref[0])


def _lru(proj, tail0, h0, cw, cb, wg_bf16, ba, bi, lam, *, stride, rows_per_step):
    b, t, _ = proj.shape
    w = cw.shape[1]
    pad = tail0.shape[1]
    r = rows_per_step
    vec = lambda: pl.BlockSpec((1, w), lambda bi_, ti: (0, 0))
    ob, h_last, tail = pl.pallas_call(
        functools.partial(_lru_kernel, stride=stride, pad=pad),
        grid=(b, t // r),
        in_specs=[
            pl.BlockSpec((1, r, w), lambda bi_, ti: (bi_, ti, 4)),
            pl.BlockSpec((1, r, w), lambda bi_, ti: (bi_, ti, 5)),
            pl.BlockSpec((1, pad, w), lambda bi_, ti: (bi_, 0, 0)),
            pl.BlockSpec((1, stride, w), lambda bi_, ti: (bi_, 0, 0)),
            pl.BlockSpec((LRU_CONV, w), lambda bi_, ti: (0, 0)),
            vec(),
            pl.BlockSpec((w, 2 * w), lambda bi_, ti: (0, 0)),
            vec(), vec(), vec(),
        ],
        out_specs=[
            pl.BlockSpec((1, r, w), lambda bi_, ti: (bi_, ti, 0)),
            pl.BlockSpec((1, stride, w), lambda bi_, ti: (bi_, 0, 0)),
            pl.BlockSpec((1, pad, w), lambda bi_, ti: (bi_, 0, 0)),
        ],
        out_shape=[
            jax.ShapeDtypeStruct((b, t, w), F32),
            jax.ShapeDtypeStruct((b, stride, w), F32),
            jax.ShapeDtypeStruct((b, pad, w), F32),
        ],
        scratch_shapes=[pltpu.VMEM((pad, w), F32), pltpu.VMEM((stride, w), F32)],
        compiler_params=_params("arbitrary", "arbitrary"),
        name="rglru",
    )(proj, proj, tail0, h0, cw, cb.reshape(1, w), wg_bf16, ba.reshape(1, w), bi.reshape(1, w),
      lam.reshape(1, w))
    return ob, h_last, tail


def _block_diag(wb):
    nb, c, _ = wb.shape
    eye = jnp.eye(nb, dtype=wb.dtype)
    return (eye[:, None, :, None] * wb[:, :, None, :]).reshape(nb * c, nb * c)


def _post_kernel(*refs, n_mix, stride, pad):
    h_ref = refs[0]
    a_refs = refs[1:1 + n_mix]
    wo_refs = refs[1 + n_mix:1 + 2 * n_mix]
    (p_ref, nf_ref, wg_ref, wu_ref, wout_ref, cw_ref, cb_ref, cst_ref, pgn_ref, wpg_ref, wpp_ref, ppn_ref,
     o_ref, fc_ref, h1_sc, xn_sc, acc_sc, carry_sc) = refs[1 + 2 * n_mix:]
    i = pl.program_id(1)
    j = pl.program_id(2)
    tm = h_ref.shape[0]

    @pl.when(j == 0)
    def _():
        y = h_ref[...]
        for a_ref, wo_ref in zip(a_refs, wo_refs):
            y = y + _dot(a_ref[...].astype(BF16), wo_ref[...])
        h1_sc[...] = y
        xn_sc[...] = _rms(y, nf_ref[...]).astype(BF16)
        acc_sc[...] = jnp.zeros(acc_sc.shape, F32)

    @pl.when(i == 0)
    def _():
        carry_sc[j] = cst_ref[0]

    xn = xn_sc[...]
    tf = wg_ref.shape[1]
    chunk = min(tf, MXU_WIDTH)
    contrib = None
    for c in range(tf // chunk):
        cols = slice(c * chunk, (c + 1) * chunk)
        g = _dot(xn, wg_ref[:, cols])
        u = _dot(xn, wu_ref[:, cols])
        xp = jnp.concatenate([carry_sc[j, :, cols], g], axis=0)
        gc = cb_ref[:, cols]
        for kk in range(FFN_CONV):
            off = pad - (FFN_CONV - 1 - kk) * stride
            gc = gc + xp[off:off + tm] * cw_ref[kk:kk + 1, cols]
        carry_sc[j, :, cols] = xp[tm:tm + pad]
        part = _dot((_gelu(gc) * u).astype(BF16), wout_ref[cols, :])
        contrib = part if contrib is None else contrib + part
    acc_sc[...] += contrib

    @pl.when(i == pl.num_programs(1) - 1)
    def _():
        fc_ref[0, j] = carry_sc[j]

    @pl.when(j == pl.num_programs(2) - 1)
    def _():
        h2 = h1_sc[...] + acc_sc[...]
        gate = jax.nn.sigmoid(_dot(_rms(h2, pgn_ref[...]).astype(BF16), wpg_ref[...]))
        e = _dot(p_ref[0].astype(BF16), wpp_ref[...])
        o_ref[...] = h2 + _rms(e, ppn_ref[...]) * gate


def _post_mixer(h, mix, wo_bf16, p, nf, wffn_in_bf16, wffn_out_bf16, cw, cb, cst, pgn, wpg_bf16, wpp_bf16,
                ppn, *, p_layer, n_seq, stride, tm, tf):
    n, d = h.shape
    dff = wffn_out_bf16.shape[0]
    nj = dff // tf
    pad = cst.shape[1]
    ni = n // n_seq // tm
    n_mix = len(mix)
    row = lambda width: pl.BlockSpec((tm, width), lambda b, i, j: (b * ni + i, 0))
    full = lambda shape: pl.BlockSpec(shape, lambda b, i, j: (0, 0))
    in_specs = (
        [row(d)] + [row(a.shape[1]) for a in mix] + [full(w_.shape) for w_ in wo_bf16]
        + [
            pl.BlockSpec((1, tm, p.shape[2]), lambda b, i, j: (p_layer, b * ni + i, 0)),
            full((1, d)),
            pl.BlockSpec((d, tf), lambda b, i, j: (0, j)),
            pl.BlockSpec((d, tf), lambda b, i, j: (0, nj + j)),
            pl.BlockSpec((tf, d), lambda b, i, j: (j, 0)),
            pl.BlockSpec((FFN_CONV, tf), lambda b, i, j: (0, j)),
            pl.BlockSpec((1, tf), lambda b, i, j: (0, j)),
            pl.BlockSpec((1, pad, tf), lambda b, i, j: (b, 0, j)),
            full((1, d)),
            full(wpg_bf16.shape),
            full(wpp_bf16.shape),
            full((1, d)),
        ])
    out, fc = pl.pallas_call(
        functools.partial(_post_kernel, n_mix=n_mix, stride=stride, pad=pad),
        grid=(n_seq, ni, nj),
        in_specs=in_specs,
        out_specs=[
            pl.BlockSpec((tm, d), lambda b, i, j: (b * ni + i, 0)),
            pl.BlockSpec((1, nj, pad, tf), lambda b, i, j: (b, 0, 0, 0)),
        ],
        out_shape=[
            jax.ShapeDtypeStruct((n, d), F32),
            jax.ShapeDtypeStruct((n_seq, nj, pad, tf), F32),
        ],
        scratch_shapes=[
            pltpu.VMEM((tm, d), F32),
            pltpu.VMEM((tm, d), BF16),
            pltpu.VMEM((tm, d), F32),
            pltpu.VMEM((nj, pad, tf), F32),
        ],
        compiler_params=_params("arbitrary", "arbitrary", "arbitrary"),
        name="post_mixer",
    )(h, *mix, *wo_bf16, p, nf.reshape(1, d), wffn_in_bf16, wffn_in_bf16, wffn_out_bf16, cw,
      cb.reshape(1, dff), cst, pgn.reshape(1, d), wpg_bf16, wpp_bf16, ppn.reshape(1, d))
    return out, jnp.transpose(fc, (0, 2, 1, 3)).reshape(n_seq, pad, dff)


def _t5_bucket(dist):
    n = jnp.maximum(dist, 0)
    max_exact = REL_BUCKETS // 2
    nf = jnp.maximum(n, max_exact).astype(F32)
    large = max_exact + (jnp.log(nf / max_exact) / math.log(REL_MAX_DIST / max_exact)
                         * (REL_BUCKETS - max_exact)).astype(jnp.int32)
    large = jnp.minimum(large, REL_BUCKETS - 1)
    return jnp.where(n < max_exact, n, large)


def _bias_table(rel_bias, dist):
    onehot = (_t5_bucket(jnp.asarray(dist))[..., None] == jnp.arange(REL_BUCKETS)).astype(F32)
    return jnp.einsum('...b,bh->h...', onehot, rel_bias.astype(F32), precision=lax.Precision.HIGHEST)


def _top3(sc, idxf, n, axis=-1):
    picks = []
    for _ in range(MOBA_TOPK):
        m = jnp.max(sc, axis=axis, keepdims=True)
        first = jnp.min(jnp.where(sc == m, idxf, float(n)), axis=axis, keepdims=True)
        picks.append((first, m))
        sc = jnp.where(idxf == first, -jnp.inf, sc)
    return picks


def _moba_prompt_kernel(pt_ref, q_ref, qn_ref, k_ref, vt_ref, bo_ref, bp_ref, bf_ref, *rest, pages_per_step):
    if pages_per_step:
        ck_ref, o_ref, kmo_ref, km_sc, mask_sc, s0_sc, s1_sc, pbuf, psem = rest
    else:
        o_ref, km_sc, mask_sc, s0_sc, s1_sc = rest
    i = pl.program_id(2)
    nb = km_sc.shape[0]
    blk = MOBA_BLOCK
    hd = C_HEAD_DIM
    ext = hd - MASK_ROWS

    if pages_per_step:
        step = (pl.program_id(0) * pl.num_programs(1) + pl.program_id(1)) * pl.num_programs(2) + i
        steps_per_seq = pt_ref.shape[1] // pages_per_step
        seq = step // steps_per_seq
        chunk = step % steps_per_seq

        def page_copy(kk):
            return pltpu.make_async_copy(ck_ref.at[pt_ref[seq, chunk * pages_per_step + kk]], pbuf.at[kk],
                                         psem.at[kk])

        for kk in range(pages_per_step):
            page_copy(kk).start()

    @pl.when(i == 0)
    def _():
        def body(jb, c):
            kb = k_ref[0, pl.ds(pl.multiple_of(jb * blk, blk), blk), :]
            km_sc[pl.ds(jb, 1), :] = jnp.sum(kb, axis=0, keepdims=True) * (1.0 / blk)
            return c
        lax.fori_loop(0, nb, body, 0)
        r = lax.broadcasted_iota(jnp.int32, (2, MASK_ROWS, blk), 1)
        mask_sc[...] = jnp.where(r < nb, NEG, 0.0)

    def mask_rows(kmh, q_t, tile):
        rowf = lax.broadcasted_iota(jnp.int32, (nb, blk), 0).astype(F32)
        sc = jnp.dot(kmh, q_t, precision=lax.Precision.HIGHEST, preferred_element_type=F32)
        sc = jnp.where(rowf < tile.astype(F32), sc, -jnp.inf)
        sel = jnp.zeros((nb, blk), F32)
        for first, m in _top3(sc, rowf, nb, axis=0):
            sel = jnp.where((rowf == first) & (m > -jnp.inf), 1.0, sel)
        mask = jnp.where(sel > 0.0, 0.0, NEG)
        if nb < MASK_ROWS:
            mask = jnp.concatenate([mask, jnp.zeros((MASK_ROWS - nb, blk), F32)], axis=0)
        return mask

    def bias_rows(bias):
        hi = bias.astype(BF16).astype(F32)
        r = lax.broadcasted_iota(jnp.int32, (ext, blk), 0)
        return jnp.where(r == 0, hi, jnp.where(r == 1, bias - hi, 0.0))

    q_t = q_ref[0].T
    c = hd ** -0.5 * LOG2E
    qp_a = jnp.concatenate([q_t[:hd] * c, mask_sc[0], bias_rows(bf_ref[0])], axis=0).astype(BF16)
    qp_b = jnp.concatenate([mask_sc[1], bias_rows(bf_ref[1]), q_t[hd:] * c], axis=0).astype(BF16)

    def logits(jb, n_blk, block_ids, far):
        start = pl.multiple_of(jb * blk, blk)
        lane = lax.broadcasted_iota(jnp.int32, (1, LANES), 1)
        lo = lax.broadcasted_iota(jnp.int32, (blk, LANES), 1) < hd
        kbs_a, kbs_b = [], []
        for g in range(n_blk):
            kb = k_ref[0, pl.ds(pl.multiple_of(start + g * blk, blk), blk), :]
            e_a = jnp.zeros((1, LANES), F32)
            e_b = jnp.zeros((1, LANES), F32)
            if block_ids is not None:
                e_a = jnp.where(lane == hd + block_ids[g], 1.0, e_a)
                e_b = jnp.where(lane == block_ids[g], 1.0, e_b)
            if far:
                e_a = jnp.where((lane == hd + MASK_ROWS) | (lane == hd + MASK_ROWS + 1), 1.0, e_a)
                e_b = jnp.where((lane == MASK_ROWS) | (lane == MASK_ROWS + 1), 1.0, e_b)
            kbs_a.append(jnp.where(lo, kb, e_a).astype(BF16))
            kbs_b.append(jnp.where(lo, e_b, kb).astype(BF16))
        kb_a = jnp.concatenate(kbs_a, axis=0) if n_blk > 1 else kbs_a[0]
        kb_b = jnp.concatenate(kbs_b, axis=0) if n_blk > 1 else kbs_b[0]
        return _dot(kb_a, qp_a), _dot(kb_b, qp_b)

    def fold(carry, s_a, s_b, jb, n_blk):
        m_a, acc_a, m_b, acc_b = carry
        keys = n_blk * blk
        vt = vt_ref[0, :, pl.ds(pl.multiple_of(jb * blk, blk), keys)]
        ones = jnp.ones((PV_ONES, keys), F32)
        vt_a = jnp.concatenate([vt[:hd], ones], axis=0).astype(BF16)
        vt_b = jnp.concatenate([vt[hd:], ones], axis=0).astype(BF16)
        mn_a = jnp.maximum(m_a, jnp.max(s_a, axis=0, keepdims=True))
        mn_b = jnp.maximum(m_b, jnp.max(s_b, axis=0, keepdims=True))
        p_a = jnp.exp2(s_a - mn_a).astype(BF16)
        p_b = jnp.exp2(s_b - mn_b).astype(BF16)
        acc_a = jnp.exp2(m_a - mn_a) * acc_a + _dot(vt_a, p_a)
        acc_b = jnp.exp2(m_b - mn_b) * acc_b + _dot(vt_b, p_b)
        return mn_a, acc_a, mn_b, acc_b

    qn_t = qn_ref[0].T
    km = km_sc[...]
    km_lo = lax.broadcasted_iota(jnp.int32, (nb, LANES), 1) < hd
    mask_next_a = mask_rows(jnp.where(km_lo, km, 0.0), qn_t, i + 1)
    mask_next_b = mask_rows(jnp.where(km_lo, 0.0, km), qn_t, i + 1)

    m0 = jnp.full((1, blk), NEG, F32)
    acc0 = jnp.zeros((hd + PV_ONES, blk), F32)
    carry = (m0, acc0, m0, acc0)
    no_mask = -LANES
    first = i == 0
    near = jnp.maximum(i - 1, 0)
    s_a, s_b = logits(near, GROUP, [jnp.where(first, no_mask, i - 1), jnp.where(first, nb - 1, no_mask)], False)
    s0_sc[0] = s_a + jnp.concatenate([jnp.where(first, bo_ref[0], bp_ref[0]), bo_ref[0]], axis=0)
    s0_sc[1] = s_b + jnp.concatenate([jnp.where(first, bo_ref[1], bp_ref[1]), bo_ref[1]], axis=0)

    n_far = jnp.maximum(i - 1, 0)
    n_grp = (n_far + GROUP - 1) // GROUP

    def far_logits(grp, s_sc):
        jb = jnp.minimum(grp * GROUP, nb - GROUP)
        ids = [jnp.where(grp * GROUP + g < n_far, grp * GROUP + g, nb - 1) for g in range(GROUP)]
        s_sc[0], s_sc[1] = logits(jb, GROUP, ids, True)

    def far_fold(cy, grp, s_sc):
        return fold(cy, s_sc[0], s_sc[1], jnp.minimum(grp * GROUP, nb - GROUP), GROUP)

    far_logits(0, s1_sc)
    carry = fold(carry, s0_sc[0], s0_sc[1], near, GROUP)
    mask_sc[0] = mask_next_a
    mask_sc[1] = mask_next_b

    def far_pair(it, cy):
        far_logits(2 * it + 1, s0_sc)
        cy = far_fold(cy, 2 * it, s1_sc)
        far_logits(2 * it + 2, s1_sc)
        return far_fold(cy, 2 * it + 1, s0_sc)

    m_a, acc_a, m_b, acc_b = lax.fori_loop(0, (n_grp + 1) // 2, far_pair, carry)
    out_t = jnp.concatenate([acc_a[:hd] / acc_a[hd:hd + 1], acc_b[:hd] / acc_b[hd:hd + 1]], axis=0)
    o_ref[0] = out_t.T

    if pages_per_step:
        @pl.when(chunk == 0)
        def _():
            kmo_ref[...] = jnp.zeros(kmo_ref.shape, F32)

        for kk in range(pages_per_step):
            page_copy(kk).wait()
        means = kmo_ref[0]
        lane = lax.broadcasted_iota(jnp.int32, means.shape, 2)
        blocks_per_step = pages_per_step // PAGES_PER_BLOCK
        for g in range(blocks_per_step):
            tot = pbuf[g * PAGES_PER_BLOCK]
            for kk in range(1, PAGES_PER_BLOCK):
                tot = tot + pbuf[g * PAGES_PER_BLOCK + kk]
            col = jnp.sum(tot, axis=-1, keepdims=True) * (1.0 / MOBA_BLOCK)
            means = jnp.where(lane == chunk * blocks_per_step + g, col, means)
        kmo_ref[0] = means


def _moba_prompt(q, k, v_t, rel_bias, paged=None):
    b, t, width = q.shape
    blk = MOBA_BLOCK
    nb = t // blk
    assert nb <= MASK_ROWS
    hp = width // LANES
    pages_per_step = 0
    page_table = jnp.zeros((1, 1), jnp.int32)
    if paged is not None:
        ck_t, pt = paged
        n_seq2, n_pages = pt.shape
        pps, rem = divmod(n_seq2 * n_pages, b * hp * nb)
        if rem == 0 and pps > 0 and pps % PAGES_PER_BLOCK == 0 and n_pages % pps == 0:
            pages_per_step, page_table = pps, pt
    ll = np.arange(blk)[:, None]
    ii = np.arange(blk)[None, :]
    bias_own = jnp.where(ii >= ll, _bias_table(rel_bias, np.maximum(ii - ll, 0)) * LOG2E, NEG)
    bias_prev = _bias_table(rel_bias, blk + ii - ll) * LOG2E
    bias_far = (_bias_table(rel_bias, np.array([2 * blk])) * LOG2E).reshape(-1, 1, 1)
    pair = lambda shape: pl.BlockSpec((2,) + shape, lambda bi, hi, qi, pt: (hi, 0, 0))
    in_specs = [
        pl.BlockSpec((1, blk, LANES), lambda bi, hi, qi, pt: (bi, qi, hi)),
        pl.BlockSpec((1, blk, LANES), lambda bi, hi, qi, pt: (bi, jnp.minimum(qi + 1, nb - 1), hi)),
        pl.BlockSpec((1, t, LANES), lambda bi, hi, qi, pt: (bi, 0, hi)),
        pl.BlockSpec((1, LANES, t), lambda bi, hi, qi, pt: (bi, hi, 0)),
        pair((blk, blk)), pair((blk, blk)), pair((1, 1)),
    ]
    out_specs = [pl.BlockSpec((1, blk, LANES), lambda bi, hi, qi, pt: (bi, qi, hi))]
    out_shape = [jax.ShapeDtypeStruct((b, t, width), F32)]
    scratch_shapes = [pltpu.VMEM((nb, LANES), F32),
                      pltpu.VMEM((2, MASK_ROWS, blk), F32),
                      pltpu.VMEM((2, GROUP * blk, blk), F32),
                      pltpu.VMEM((2, GROUP * blk, blk), F32)]
    args = [page_table, q, q, k, v_t, bias_own, bias_prev, bias_far]
    if pages_per_step:
        _, h2, dh2, page = ck_t.shape
        steps_per_seq = n_pages // pages_per_step
        in_specs.append(pl.BlockSpec(memory_space=pl.ANY))
        out_specs.append(pl.BlockSpec((1, h2, dh2, n_pages // PAGES_PER_BLOCK),
                                      lambda bi, hi, qi, pt: (((bi * hp + hi) * nb + qi) // steps_per_seq, 0, 0, 0)))
        out_shape.append(jax.ShapeDtypeStruct((n_seq2, h2, dh2, n_pages // PAGES_PER_BLOCK), F32))
        scratch_shapes += [pltpu.VMEM((pages_per_step, h2, dh2, page), F32),
                           pltpu.SemaphoreType.DMA((pages_per_step,))]
        args.append(ck_t)
    outs = pl.pallas_call(
        functools.partial(_moba_prompt_kernel, pages_per_step=pages_per_step),
        grid_spec=pltpu.PrefetchScalarGridSpec(
            num_scalar_prefetch=1, grid=(b, hp, nb), in_specs=in_specs, out_specs=out_specs,
            scratch_shapes=scratch_shapes),
        out_shape=out_shape,
        compiler_params=_params("arbitrary", "arbitrary", "arbitrary"),
        name="moba_prompt",
    )(*args)
    return (outs[0], outs[1]) if pages_per_step else (outs[0], None)


def _kmean_kernel(pt_ref, *refs):
    page_refs, o_ref = refs[:-1], refs[-1]
    j = pl.program_id(1)

    @pl.when(j == 0)
    def _():
        o_ref[...] = jnp.zeros(o_ref.shape, F32)

    lane = lax.broadcasted_iota(jnp.int32, o_ref.shape[1:], 2)
    out = o_ref[0]
    for g in range(len(page_refs) // PAGES_PER_BLOCK):
        pages = page_refs[g * PAGES_PER_BLOCK:(g + 1) * PAGES_PER_BLOCK]
        tot = pages[0][0]
        for pr in pages[1:]:
            tot = tot + pr[0]
        col = jnp.sum(tot, axis=-1, keepdims=True) * (1.0 / MOBA_BLOCK)
        out = jnp.where(lane == j * KMEAN_BLOCKS_PER_STEP + g, col, out)
    o_ref[0] = out


def _block_means(ck_t, page_table, nb):
    _, h, dh, page = ck_t.shape
    b = page_table.shape[0]
    g = KMEAN_BLOCKS_PER_STEP
    assert nb % g == 0
    n_pages = g * PAGES_PER_BLOCK
    page_spec = lambda k: pl.BlockSpec((1, h, dh, page), lambda bi, j, pt: (pt[bi, n_pages * j + k], 0, 0, 0))
    return pl.pallas_call(
        _kmean_kernel,
        grid_spec=pltpu.PrefetchScalarGridSpec(
            num_scalar_prefetch=1,
            grid=(b, nb // g),
            in_specs=[page_spec(k) for k in range(n_pages)],
            out_specs=pl.BlockSpec((1, h, dh, nb), lambda bi, j, pt: (bi, 0, 0, 0)),
        ),
        out_shape=jax.ShapeDtypeStruct((b, h, dh, nb), F32),
        compiler_params=_params("arbitrary", "arbitrary"),
        name="moba_block_means",
    )(page_table, *([ck_t] * n_pages))


def _select_kernel(qt_ref, km_ref, o_ref):
    s_len, _, h = qt_ref.shape[1:]
    nb = km_ref.shape[3]
    rows = s_len * h
    rid = lax.broadcasted_iota(jnp.int32, (rows, nb), 0)
    sc = jnp.zeros((rows, nb), F32)
    for s in range(s_len):
        qt = qt_ref[0, s]
        for hh in range(h):
            sc = jnp.where(rid == s * h + hh, jnp.sum(km_ref[0, hh] * qt[:, hh:hh + 1], axis=0, keepdims=True), sc)
    colf = lax.broadcasted_iota(jnp.int32, (rows, nb), 1).astype(F32)
    picks = [jnp.minimum(first, nb - 1.0).astype(jnp.int32) for first, _ in _top3(sc, colf, nb)]
    lane = lax.broadcasted_iota(jnp.int32, (rows, LANES), 1)
    o_ref[0] = jnp.where(lane == 0, picks[0], jnp.where(lane == 1, picks[1], picks[2]))


def _select_blocks(q_t, kmean_t):
    b, s_len, dh, h = q_t.shape
    nb = kmean_t.shape[3]
    out = pl.pallas_call(
        _select_kernel,
        grid=(b,),
        in_specs=[
            pl.BlockSpec((1, s_len, dh, h), lambda bi: (bi, 0, 0, 0)),
            pl.BlockSpec((1, h, dh, nb), lambda bi: (bi, 0, 0, 0)),
        ],
        out_specs=pl.BlockSpec((1, s_len * h, LANES), lambda bi: (bi, 0, 0)),
        out_shape=jax.ShapeDtypeStruct((b, s_len * h, LANES), jnp.int32),
        compiler_params=_params("arbitrary"),
        name="moba_select",
    )(q_t, kmean_t)
    return out[:, :, :MOBA_TOPK].reshape(b, s_len, h, MOBA_TOPK)


def _moba_sample_kernel(pt_ref, sel_ref, ck_ref, cv_ref, qt_ref, kn_ref, vn_ref, bl_ref, bo_ref, bf_ref,
                        o_ref, kbuf, vbuf, sem, *, nb):
    b = pl.program_id(0)
    s = pl.program_id(1)
    s_len = pl.num_programs(1)
    _, h, _, dh, _ = kbuf.shape
    ppb = PAGES_PER_BLOCK
    step = b * s_len + s
    cur = step % 2

    def copies(st, buf):
        out = []
        for hh in range(h):
            for n in range(MOBA_TOPK):
                idx = sel_ref[(st * h + hh) * MOBA_TOPK + n]
                for pg in range(ppb):
                    phys = pt_ref[st // s_len, idx * ppb + pg]
                    slot = n * ppb + pg
                    out.append(pltpu.make_async_copy(ck_ref.at[phys, hh], kbuf.at[buf, hh, slot], sem.at[buf, 0]))
                    out.append(pltpu.make_async_copy(cv_ref.at[phys, hh], vbuf.at[buf, hh, slot], sem.at[buf, 1]))
        return out

    @pl.when(step == 0)
    def _():
        for cp in copies(step, cur):
            cp.start()

    @pl.when(step + 1 < pl.num_programs(0) * s_len)
    def _():
        for cp in copies(step + 1, 1 - cur):
            cp.start()

    for cp in copies(step, cur):
        cp.wait()

    kbuf = kbuf.at[cur]
    vbuf = vbuf.at[cur]
    lane = lax.broadcasted_iota(jnp.int32, (dh, LANES), 1)
    qt = qt_ref[0, 0] * (dh ** -0.5)
    out = jnp.zeros((dh, LANES), F32)
    for hh in range(h):
        qc = qt[:, hh:hh + 1]
        far = bf_ref[hh:hh + 1, :]
        logits = []
        for n in range(MOBA_TOPK):
            idx = sel_ref[((b * s_len + s) * h + hh) * MOBA_TOPK + n]
            for pg in range(ppb):
                lg = jnp.sum(kbuf[hh, n * ppb + pg] * qc, axis=0, keepdims=True)
                logits.append(lg + jnp.where(idx == nb - 1, bl_ref[0, hh, pg:pg + 1, :], far))
        lown = jnp.sum(kn_ref[0, hh] * qc, axis=0, keepdims=True) + bo_ref[0, hh:hh + 1, :]
        m = jnp.max(lown, axis=-1, keepdims=True)
        for lg in logits:
            m = jnp.maximum(m, jnp.max(lg, axis=-1, keepdims=True))
        pown = jnp.exp(lown - m)
        den = jnp.sum(pown, axis=-1, keepdims=True)
        col = jnp.sum(vn_ref[0, hh] * pown, axis=-1, keepdims=True)
        for slot, lg in enumerate(logits):
            p = jnp.exp(lg - m)
            den = den + jnp.sum(p, axis=-1, keepdims=True)
            col = col + jnp.sum(vbuf[hh, slot] * p, axis=-1, keepdims=True)
        out = jnp.where(lane == hh, col / den, out)
    o_ref[0, 0] = out


def _moba_sample(q, k_new, v_new, ck_t, cv_t, page_table, rel_bias, past_len, kmean_t=None):
    b, s_len, h, dh = q.shape
    page = ck_t.shape[3]
    blk = MOBA_BLOCK
    assert past_len % blk == 0 and blk == PAGES_PER_BLOCK * page
    nb = past_len // blk
    assert nb >= MOBA_TOPK
    q_t = jnp.transpose(q, (0, 1, 3, 2))
    if kmean_t is None:
        kmean_t = _block_means(ck_t, page_table, nb)
    sel = _select_blocks(q_t, kmean_t)

    ss = np.arange(s_len)
    bias_last = _bias_table(rel_bias, blk + ss[:, None] - np.arange(blk)[None, :])
    bias_last = jnp.transpose(bias_last, (1, 0, 2)).reshape(s_len, h, PAGES_PER_BLOCK, page)
    own_d = ss[:, None] - ss[None, :]
    bias_own = jnp.where(own_d >= 0, _bias_table(rel_bias, np.maximum(own_d, 0)), NEG)
    bias_own = jnp.transpose(bias_own, (1, 0, 2))
    bias_far = _bias_table(rel_bias, np.array([2 * blk]))
    kn_t = jnp.transpose(k_new, (0, 2, 3, 1))
    vn_t = jnp.transpose(v_new, (0, 2, 3, 1))

    out = pl.pallas_call(
        functools.partial(_moba_sample_kernel, nb=nb),
        grid_spec=pltpu.PrefetchScalarGridSpec(
            num_scalar_prefetch=2,
            grid=(b, s_len),
            in_specs=[
                pl.BlockSpec(memory_space=pl.ANY),
                pl.BlockSpec(memory_space=pl.ANY),
                pl.BlockSpec((1, 1, dh, h), lambda bi, si, pt, sl: (bi, si, 0, 0)),
                pl.BlockSpec((1, h, dh, s_len), lambda bi, si, pt, sl: (bi, 0, 0, 0)),
                pl.BlockSpec((1, h, dh, s_len), lambda bi, si, pt, sl: (bi, 0, 0, 0)),
                pl.BlockSpec((1, h, PAGES_PER_BLOCK, page), lambda bi, si, pt, sl: (si, 0, 0, 0)),
                pl.BlockSpec((1, h, s_len), lambda bi, si, pt, sl: (si, 0, 0)),
                pl.BlockSpec((h, 1), lambda bi, si, pt, sl: (0, 0)),
            ],
            out_specs=pl.BlockSpec((1, 1, dh, LANES), lambda bi, si, pt, sl: (bi, si, 0, 0)),
            scratch_shapes=[
                pltpu.VMEM((2, h, PAGES_PER_BLOCK * MOBA_TOPK, dh, page), F32),
                pltpu.VMEM((2, h, PAGES_PER_BLOCK * MOBA_TOPK, dh, page), F32),
                pltpu.SemaphoreType.DMA((2, 2)),
            ],
        ),
        out_shape=jax.ShapeDtypeStruct((b, s_len, dh, LANES), F32),
        compiler_params=_params("arbitrary", "arbitrary"),
        name="moba_sample",
    )(page_table, sel.reshape(-1), ck_t, cv_t, q_t, kn_t, vn_t, bias_last, bias_own, bias_far)
    return jnp.transpose(out[:, :, :, :h], (0, 1, 3, 2))


def _prep_weights(w):
    bf = lambda a: a.astype(BF16)
    n_even = w['w0_in'].shape[0]
    n_odd = w['c_in'].shape[0]
    out = dict(w)
    out['w0_in'] = bf(w['w0_in'])
    out['w0_out'] = bf(w['w0_out'])
    out['lru_wg'] = [bf(jnp.concatenate([_block_diag(w['lru_wa'][e]), _block_diag(w['lru_wi'][e])], axis=1))
                     for e in range(n_even)]
    out['c_in'] = bf(w['c_in'])
    out['c_out'] = bf(w['c_out'])
    out['c_gain'] = [jnp.concatenate([jnp.tile(w['c_qn'][o], C_HEADS), jnp.tile(w['c_kn'][o], C_HEADS),
                                      jnp.ones((C_HEADS * C_HEAD_DIM,), F32)]).reshape(1, -1)
                     for o in range(n_odd)]
    out['ffn_in'] = bf(w['ffn_in'])
    out['ffn_out'] = bf(w['ffn_out'])
    out['ple_proj'] = bf(w['ple_proj'])
    out['ple_gate'] = bf(w['ple_gate'])
    return out


def _post(w, l, h, mix, wo, p, p_layer, cst, *, n_seq, stride, tm, tf):
    return _post_mixer(h, mix, wo, p, w['norm_ffn'][l], w['ffn_in'][l], w['ffn_out'][l], w['ffn_conv_w'][l],
                       w['ffn_conv_b'][l], cst, w['ple_gate_norm'][l], w['ple_gate'][l], w['ple_proj'][l],
                       w['ple_post_norm'][l], p_layer=p_layer, n_seq=n_seq, stride=stride, tm=tm, tf=tf)


def _prompt_trunk(x, p, w, rel_bias, ck_ts, page_table):
    b, t, d = x.shape
    n = b * t
    depth = p.shape[0]
    h = x.reshape(n, d)
    tm = 512
    rets, lrus, lconvs, fconvs, ks, vs, kmeans = [], [], [], [], [], [], []
    lw = w['lru_conv_w'].shape[2]
    dff = w['ffn_out'].shape[1]
    for l in range(depth):
        cst = jnp.zeros((b, SUBLANES, dff), F32)
        if l % 2 == 0:
            e = l // 2
            proj = _norm_proj(h, w['norm_mix'][l], w['w0_in'][e], tm=512, tn=1024).reshape(b, t, -1)
            o, s_new = _retention(proj, jnp.zeros((b, RET_HEADS, RET_DIM, RET_DIM), F32), jnp.arange(t),
                                  w['ret_gn_g'][e], w['ret_gn_b'][e], c_eff=RET_CHUNK, c_pad=RET_CHUNK,
                                  rows_per_step=512)
            ob, h_last, tail = _lru(proj, jnp.zeros((b, SUBLANES, lw), F32), jnp.zeros((b, 1, lw), F32),
                                    w['lru_conv_w'][e], w['lru_conv_b'][e], w['lru_wg'][e], w['lru_ba'][e],
                                    w['lru_bi'][e], w['lru_lam'][e], stride=1, rows_per_step=512)
            rets.append(s_new)
            lrus.append(h_last[:, 0])
            lconvs.append(tail[:, SUBLANES - (LRU_CONV - 1):])
            rw = o.shape[-1]
            mix = [o.reshape(n, rw), ob.reshape(n, lw)]
            wo = [w['w0_out'][e][:rw], w['w0_out'][e][rw:]]
        else:
            o_ = l // 2
            q, k, k_t, v_t = _norm_proj(h, w['norm_mix'][l], w['c_in'][o_], tm=512, tn=1024,
                                        head_gain=w['c_gain'][o_], n_rms_tiles=2,
                                        plan=((False,), (False, True), (True,)), n_seq=b)
            cw = q.shape[1]
            att, kmean_t = _moba_prompt(q.reshape(b, t, cw), k.reshape(b, t, cw), v_t, rel_bias,
                                        paged=(ck_ts[o_], page_table))
            kmeans.append(kmean_t)
            heads = lambda a: jnp.transpose(a.reshape(b, C_HEADS, C_HEAD_DIM, t), (0, 3, 1, 2))
            ks.append(heads(k_t))
            vs.append(heads(v_t))
            mix = [att.reshape(n, cw)]
            wo = [w['c_out'][o_]]
        h, fc = _post(w, l, h, mix, wo, p.reshape(depth, n, -1), l, cst, n_seq=b, stride=1, tm=tm, tf=512)
        fconvs.append(fc[:, SUBLANES - (FFN_CONV - 1):])
    return (h.reshape(b, t, d), jnp.stack(rets), jnp.stack(lrus), jnp.stack(lconvs), jnp.stack(fconvs),
            jnp.stack(ks), jnp.stack(vs), kmeans)


def _sample_trunk(x, p, ret0, lru0, lconv0, fconv0, ck_ts, cv_ts, page_table, w, rel_bias, kmeans):
    b, t, d = x.shape
    n = b * t
    depth = p.shape[0]
    past_len = page_table.shape[1] * ck_ts[0].shape[3]
    tmaj = lambda a: jnp.transpose(a, (1, 0) + tuple(range(2, a.ndim)))
    h = tmaj(x).reshape(n, d)
    rets, lrus, lconvs, fconvs, ks, vs = [], [], [], [], [], []
    lw = w['lru_conv_w'].shape[2]
    for l in range(depth):
        cst = tmaj(fconv0[l]).reshape(1, (FFN_CONV - 1) * b, -1)
        if l % 2 == 0:
            e = l // 2
            proj = _norm_proj(h, w['norm_mix'][l], w['w0_in'][e], tm=n, tn=1024)
            rcols = 4 * RET_HEADS * RET_DIM
            pr = tmaj(proj[:, :rcols].reshape(t, b, rcols))
            pr = jnp.pad(pr, ((0, 0), (0, SUBLANES - t), (0, 0)))
            pos = past_len + jnp.arange(SUBLANES)
            o, s_new = _retention(pr, ret0[e].astype(F32), pos, w['ret_gn_g'][e], w['ret_gn_b'][e],
                                  c_eff=t, c_pad=SUBLANES, rows_per_step=SUBLANES)
            o = tmaj(o[:, :t]).reshape(n, -1)
            tail0 = tmaj(lconv0[e]).reshape(1, (LRU_CONV - 1) * b, lw)
            ob, h_last, tail = _lru(proj.reshape(1, n, -1), tail0, lru0[e].reshape(1, b, lw),
                                    w['lru_conv_w'][e], w['lru_conv_b'][e], w['lru_wg'][e], w['lru_ba'][e],
                                    w['lru_bi'][e], w['lru_lam'][e], stride=b, rows_per_step=n)
            rets.append(s_new)
            lrus.append(h_last[0])
            lconvs.append(tmaj(tail.reshape(LRU_CONV - 1, b, lw)))
            rw = o.shape[-1]
            mix = [o, ob.reshape(n, lw)]
            wo = [w['w0_out'][e][:rw], w['w0_out'][e][rw:]]
        else:
            o_ = l // 2
            q, k, v = _norm_proj(h, w['norm_mix'][l], w['c_in'][o_], tm=n, tn=1024, head_gain=w['c_gain'][o_],
                                 n_rms_tiles=2, plan=((False,), (False,), (False,)))
            heads = lambda a: tmaj(a.reshape(t, b, C_HEADS, C_HEAD_DIM))
            q, k, v = heads(q), heads(k), heads(v)
            att = _moba_sample(q, k, v, ck_ts[o_], cv_ts[o_], page_table, rel_bias, past_len, kmean_t=kmeans[o_])
            ks.append(k)
            vs.append(v)
            mix = [tmaj(att).reshape(n, -1)]
            wo = [w['c_out'][o_]]
        h, fc = _post(w, l, h, mix, wo, tmaj(p[l]).reshape(1, n, -1), 0, cst, n_seq=1, stride=b, tm=n, tf=512)
        fconvs.append(tmaj(fc.reshape(FFN_CONV - 1, b, -1)))
    return (tmaj(h.reshape(t, b, d)), jnp.stack(rets), jnp.stack(lrus), jnp.stack(lconvs), jnp.stack(fconvs),
            jnp.stack(ks), jnp.stack(vs))


def kernel(x_prompt, x_sample, state_ret, state_lru, state_lru_conv, state_ffn_conv, cache_k, cache_v, page_table, p_prompt, p_sample, norm_mix, norm_ffn, w0_in, ret_gn_g, ret_gn_b, lru_conv_w, lru_conv_b, lru_wa, lru_ba, lru_wi, lru_bi, lru_lam, w0_out, c_in, c_qn, c_kn, c_out, rel_bias, ffn_in, ffn_conv_w, ffn_conv_b, ffn_out, ple_proj, ple_gate_norm, ple_gate, ple_post_norm):
    w = _prep_weights(dict(
        norm_mix=norm_mix, norm_ffn=norm_ffn, w0_in=w0_in, ret_gn_g=ret_gn_g, ret_gn_b=ret_gn_b,
        lru_conv_w=lru_conv_w, lru_conv_b=lru_conv_b, lru_wa=lru_wa, lru_ba=lru_ba, lru_wi=lru_wi,
        lru_bi=lru_bi, lru_lam=lru_lam, w0_out=w0_out, c_in=c_in, c_qn=c_qn, c_kn=c_kn, c_out=c_out,
        ffn_in=ffn_in, ffn_conv_w=ffn_conv_w, ffn_conv_b=ffn_conv_b, ffn_out=ffn_out, ple_proj=ple_proj,
        ple_gate_norm=ple_gate_norm, ple_gate=ple_gate, ple_post_norm=ple_post_norm))
    ck_ts = [jnp.transpose(cache_k[o], (0, 2, 3, 1)) for o in range(cache_k.shape[0])]
    cv_ts = [jnp.transpose(cache_v[o], (0, 2, 3, 1)) for o in range(cache_v.shape[0])]
    y_p, ret_p, lru_p, lconv_p, fconv_p, k_p, v_p, kmeans = _prompt_trunk(
        x_prompt, p_prompt, w, rel_bias, ck_ts, page_table)
    y_s, ret_s, lru_s, lconv_s, fconv_s, k_s, v_s = _sample_trunk(
        x_sample, p_sample, state_ret, state_lru, state_lru_conv, state_ffn_conv, ck_ts, cv_ts, page_table,
        w, rel_bias, kmeans)
    return (y_p, y_s, ret_p, ret_s, lru_p, lru_s, lconv_p, lconv_s, fconv_p, fconv_s, k_p, k_s, v_p, v_s)
```

```python
import functools
import math

import numpy as np
import jax
import jax.numpy as jnp
from jax import lax
from jax.experimental import pallas as pl
from jax.experimental.pallas import tpu as pltpu

F32 = jnp.float32
BF16 = jnp.bfloat16
EPS = 1e-6
NEG = -1e30

LANES = 128
SUBLANES = 8
MXU_WIDTH = 256
VMEM_LIMIT_BYTES = 48 * 1024 * 1024

RET_HEADS = 4
RET_DIM = 128
RET_CHUNK = 128
ROPE_THETA = 10000.0
LRU_C = 8.0
LRU_CONV = 4
FFN_CONV = 3
C_HEADS = 16
C_HEAD_DIM = 64
MOBA_BLOCK = 256
MOBA_TOPK = 3
REL_BUCKETS = 32
REL_MAX_DIST = 128
LOG2E = 1.4426950408889634
MASK_ROWS = 32
PV_ONES = 16
GROUP = 2
KMEAN_BLOCKS_PER_STEP = 4
PAGES_PER_BLOCK = 2


def _params(*sem):
    return pltpu.CompilerParams(dimension_semantics=sem, vmem_limit_bytes=VMEM_LIMIT_BYTES)


def _rms(x, g):
    return x * lax.rsqrt(jnp.mean(x * x, axis=-1, keepdims=True) + EPS) * g


def _gelu(x):
    return 0.5 * x * (1.0 + jnp.tanh(0.7978845608028654 * (x + 0.044715 * (x * x * x))))


def _dot(a, b):
    return jnp.dot(a, b, preferred_element_type=F32)


def _dot_nt(a, b):
    return lax.dot_general(a, b, (((1,), (1,)), ((), ())), preferred_element_type=F32)


def _dot_tn(a, b):
    return lax.dot_general(a, b, (((0,), (0,)), ((), ())), preferred_element_type=F32)


def _head_rms(y, hd):
    tm, tn = y.shape
    lo = lax.broadcasted_iota(jnp.int32, (tm, LANES), 1) < hd
    outs = []
    for s in range(tn // LANES):
        ys = y[:, s * LANES:(s + 1) * LANES]
        y2 = ys * ys
        s0 = jnp.sum(jnp.where(lo, y2, 0.0), axis=-1, keepdims=True)
        s1 = jnp.sum(jnp.where(lo, 0.0, y2), axis=-1, keepdims=True)
        r = lax.rsqrt(jnp.where(lo, s0, s1) * (1.0 / hd) + EPS)
        outs.append(ys * r)
    return jnp.concatenate(outs, axis=-1)


def _norm_proj_kernel(x_ref, g_ref, w_ref, hg_ref, *o_refs, n_rms_tiles, plan, tn):
    nout = w_ref.shape[1]
    xn = _rms(x_ref[...], g_ref[...]).astype(BF16)
    k = 0
    for t in range(nout // tn):
        cols = slice(t * tn, (t + 1) * tn)
        y = _dot(xn, w_ref[:, cols])
        if not plan:
            o_refs[0][:, cols] = y
            continue
        if t < n_rms_tiles:
            y = _head_rms(y, C_HEAD_DIM) * hg_ref[:, cols]
        for transposed in plan[t]:
            if transposed:
                o_refs[k][0] = y.T
            else:
                o_refs[k][...] = y
            k += 1


def _norm_proj(x, g, w_bf16, *, tm, tn, head_gain=None, n_rms_tiles=0, plan=None, n_seq=1):
    n, d = x.shape
    nout = w_bf16.shape[1]
    nj = nout // tn
    if head_gain is None:
        head_gain = jnp.ones((1, nout), F32)
    if plan is None:
        out_shape = [jax.ShapeDtypeStruct((n, nout), F32)]
        out_specs = [pl.BlockSpec((tm, nout), lambda i: (i, 0))]
    else:
        assert len(plan) == nj
        t_len = n // n_seq
        nt = t_len // tm
        out_shape, out_specs = [], []
        for outs in plan:
            for transposed in outs:
                if transposed:
                    out_shape.append(jax.ShapeDtypeStruct((n_seq, tn, t_len), F32))
                    out_specs.append(pl.BlockSpec((1, tn, tm), lambda i: (i // nt, 0, i % nt)))
                else:
                    out_shape.append(jax.ShapeDtypeStruct((n, tn), F32))
                    out_specs.append(pl.BlockSpec((tm, tn), lambda i: (i, 0)))
    outs = pl.pallas_call(
        functools.partial(_norm_proj_kernel, n_rms_tiles=n_rms_tiles, plan=plan, tn=tn),
        grid=(n // tm,),
        in_specs=[
            pl.BlockSpec((tm, d), lambda i: (i, 0)),
            pl.BlockSpec((1, d), lambda i: (0, 0)),
            pl.BlockSpec((d, nout), lambda i: (0, 0)),
            pl.BlockSpec((1, nout), lambda i: (0, 0)),
        ],
        out_specs=out_specs,
        out_shape=out_shape,
        compiler_params=_params("arbitrary"),
        name="norm_proj",
    )(x, g.reshape(1, d), w_bf16, head_gain)
    return outs[0] if plan is None else outs


def _ret_kernel(q_ref, k_ref, v_ref, g_ref, s0_ref, cos_ref, sin_ref, dm_ref, qd_ref, kd_ref, cd_ref,
                gg_ref, gb_ref, o_ref, s_ref, st_ref, *, c, nsub):
    t = pl.program_id(1)

    @pl.when(t == 0)
    def _():
        st_ref[...] = s0_ref[0]

    for u in range(nsub):
        rows = slice(u * c, (u + 1) * c)
        cos = cos_ref[rows, :]
        sin = sin_ref[rows, :]
        for hh in range(RET_HEADS):
            cols = slice(hh * RET_DIM, (hh + 1) * RET_DIM)
            q = q_ref[0, rows, cols]
            k = k_ref[0, rows, cols]
            q = q * cos + pltpu.roll(q, RET_DIM // 2, 1) * sin
            k = (k * cos + pltpu.roll(k, RET_DIM // 2, 1) * sin) * (RET_DIM ** -0.5)
            s = st_ref[hh]
            qb = q.astype(BF16)
            vb = v_ref[0, rows, cols].astype(BF16)
            att = _dot_nt(qb, k.astype(BF16)) * dm_ref[hh]
            o = _dot(att.astype(BF16), vb) + _dot(qb, s.astype(BF16)) * qd_ref[hh]
            st_ref[hh] = s * cd_ref[hh] + _dot_tn((k * kd_ref[hh]).astype(BF16), vb)
            mu = jnp.mean(o, axis=-1, keepdims=True)
            oc = o - mu
            var = jnp.mean(oc * oc, axis=-1, keepdims=True)
            o = oc * lax.rsqrt(var + EPS) * gg_ref[:, cols] + gb_ref[:, cols]
            gate = g_ref[0, rows, cols]
            o_ref[0, rows, cols] = o * (gate * jax.nn.sigmoid(gate))

    @pl.when(t == pl.num_programs(1) - 1)
    def _():
        s_ref[0] = st_ref[...]


def _retention(proj, s0, pos, gn_g, gn_b, *, c_eff, c_pad, rows_per_step):
    b, t, _ = proj.shape
    h = RET_HEADS
    half = RET_DIM // 2
    inv = ROPE_THETA ** (-jnp.arange(half, dtype=F32) / half)
    ang = pos.astype(F32)[:, None] * inv[None, :]
    cos2 = jnp.concatenate([jnp.cos(ang), jnp.cos(ang)], axis=-1)
    sin2 = jnp.concatenate([-jnp.sin(ang), jnp.sin(ang)], axis=-1)
    log_g = jnp.log1p(-jnp.exp2(-5.0 - jnp.arange(h, dtype=F32)))
    idx = jnp.arange(c_pad, dtype=F32)
    diff = idx[:, None] - idx[None, :]
    dmask = jnp.where(diff >= 0, jnp.exp(log_g[:, None, None] * jnp.maximum(diff, 0.0)), 0.0)
    q_dec = jnp.exp(log_g[:, None, None] * (idx[None, :, None] + 1.0))
    k_dec = jnp.exp(log_g[:, None, None] * (c_eff - 1.0 - idx[None, :, None]))
    c_dec = jnp.exp(log_g * c_eff).reshape(h, 1, 1)
    r = rows_per_step
    nsub = r // c_pad
    w = h * RET_DIM
    col = lambda part: pl.BlockSpec((1, r, w), lambda bi, ti, part=part: (bi, ti, part))
    const = lambda shape: pl.BlockSpec(shape, lambda bi, ti: (0,) * len(shape))
    o, s_new = pl.pallas_call(
        functools.partial(_ret_kernel, c=c_pad, nsub=nsub),
        grid=(b, t // r),
        in_specs=[
            col(0), col(1), col(2), col(3),
            pl.BlockSpec((1, h, RET_DIM, RET_DIM), lambda bi, ti: (bi, 0, 0, 0)),
            pl.BlockSpec((r, RET_DIM), lambda bi, ti: (ti, 0)),
            pl.BlockSpec((r, RET_DIM), lambda bi, ti: (ti, 0)),
            const((h, c_pad, c_pad)), const((h, c_pad, 1)), const((h, c_pad, 1)), const((h, 1, 1)),
            const((1, w)), const((1, w)),
        ],
        out_specs=[
            pl.BlockSpec((1, r, w), lambda bi, ti: (bi, ti, 0)),
            pl.BlockSpec((1, h, RET_DIM, RET_DIM), lambda bi, ti: (bi, 0, 0, 0)),
        ],
        out_shape=[
            jax.ShapeDtypeStruct((b, t, w), F32),
            jax.ShapeDtypeStruct((b, h, RET_DIM, RET_DIM), F32),
        ],
        scratch_shapes=[pltpu.VMEM((h, RET_DIM, RET_DIM), F32)],
        compiler_params=_params("arbitrary", "arbitrary"),
        name="retention",
    )(proj, proj, proj, proj, s0, cos2, sin2, dmask, q_dec, k_dec, c_dec,
      gn_g.reshape(1, -1), gn_b.reshape(1, -1))
    return o, s_new


def _shift_rows(x, shift, fill):
    if shift % SUBLANES == 0:
        return jnp.concatenate([jnp.full((shift, x.shape[1]), fill, x.dtype), x[:-shift]], axis=0)
    rows = lax.broadcasted_iota(jnp.int32, x.shape, 0)
    return jnp.where(rows >= shift, pltpu.roll(x, shift, 0), fill)


def _lru_kernel(bx_ref, bg_ref, tail0_ref, h0_ref, cw_ref, cb_ref, wg_ref, ba_ref, bi_ref, lam_ref,
                ob_ref, hl_ref, tail_ref, tail_sc, h_sc, *, stride, pad):
    t = pl.program_id(1)
    r, w = bx_ref.shape[1], bx_ref.shape[2]

    @pl.when(t == 0)
    def _():
        tail_sc[...] = tail0_ref[0]
        h_sc[...] = h0_ref[0]

    bx = bx_ref[0]
    xp = jnp.concatenate([tail_sc[...], bx], axis=0)
    xc = cb_ref[...]
    for j in range(LRU_CONV):
        off = pad - (LRU_CONV - 1 - j) * stride
        xc = xc + xp[off:off + r] * cw_ref[j:j + 1, :]
    new_tail = xp[r:r + pad]
    tail_sc[...] = new_tail
    tail_ref[0] = new_tail

    z = _dot(xc.astype(BF16), wg_ref[...])
    rg = jax.nn.sigmoid(z[:, :w] + ba_ref[...])
    ig = jax.nn.sigmoid(z[:, w:] + bi_ref[...])
    nl = -lam_ref[...]
    softplus = jnp.maximum(nl, 0.0) + jnp.log1p(jnp.exp(-jnp.abs(nl)))
    log_a = -LRU_C * rg * softplus
    a = jnp.exp(log_a)
    th = jnp.tanh(log_a)
    u = jnp.sqrt(-2.0 * th / (1.0 - th)) * (ig * xc)

    d = 1
    while d < r // stride:
        u = u + a * _shift_rows(u, d * stride, 0.0)
        a = a * _shift_rows(a, d * stride, 1.0)
        d *= 2
    hc = h_sc[...]
    if r > stride:
        hc = jnp.tile(hc, (r // stride, 1)) if stride > 1 else jnp.broadcast_to(hc, (r, w))
    hs = u + a * hc
    h_last = hs[r - stride:, :]
    h_sc[...] = h_last
    hl_ref[0] = h_last
    ob_ref[0] = hs * _gelu(bg_ref[0])


def _lru(proj, tail0, h0, cw, cb, wg_bf16, ba, bi, lam, *, stride, rows_per_step):
    b, t, _ = proj.shape
    w = cw.shape[1]
    pad = tail0.shape[1]
    r = rows_per_step
    vec = lambda: pl.BlockSpec((1, w), lambda bi_, ti: (0, 0))
    ob, h_last, tail = pl.pallas_call(
        functools.partial(_lru_kernel, stride=stride, pad=pad),
        grid=(b, t // r),
        in_specs=[
            pl.BlockSpec((1, r, w), lambda bi_, ti: (bi_, ti, 4)),
            pl.BlockSpec((1, r, w), lambda bi_, ti: (bi_, ti, 5)),
            pl.BlockSpec((1, pad, w), lambda bi_, ti: (bi_, 0, 0)),
            pl.BlockSpec((1, stride, w), lambda bi_, ti: (bi_, 0, 0)),
            pl.BlockSpec((LRU_CONV, w), lambda bi_, ti: (0, 0)),
            vec(),
            pl.BlockSpec((w, 2 * w), lambda bi_, ti: (0, 0)),
            vec(), vec(), vec(),
        ],
        out_specs=[
            pl.BlockSpec((1, r, w), lambda bi_, ti: (bi_, ti, 0)),
            pl.BlockSpec((1, stride, w), lambda bi_, ti: (bi_, 0, 0)),
            pl.BlockSpec((1, pad, w), lambda bi_, ti: (bi_, 0, 0)),
        ],
        out_shape=[
            jax.ShapeDtypeStruct((b, t, w), F32),
            jax.ShapeDtypeStruct((b, stride, w), F32),
            jax.ShapeDtypeStruct((b, pad, w), F32),
        ],
        scratch_shapes=[pltpu.VMEM((pad, w), F32), pltpu.VMEM((stride, w), F32)],
        compiler_params=_params("arbitrary", "arbitrary"),
        name="rglru",
    )(proj, proj, tail0, h0, cw, cb.reshape(1, w), wg_bf16, ba.reshape(1, w), bi.reshape(1, w),
      lam.reshape(1, w))
    return ob, h_last, tail


def _block_diag(wb):
    nb, c, _ = wb.shape
    eye = jnp.eye(nb, dtype=wb.dtype)
    return (eye[:, None, :, None] * wb[:, :, None, :]).reshape(nb * c, nb * c)


def _post_kernel(*refs, n_mix, stride, pad):
    h_ref = refs[0]
    a_refs = refs[1:1 + n_mix]
    wo_refs = refs[1 + n_mix:1 + 2 * n_mix]
    (p_ref, nf_ref, wg_ref, wu_ref, wout_ref, cw_ref, cb_ref, cst_ref, pgn_ref, wpg_ref, wpp_ref, ppn_ref,
     o_ref, fc_ref, h1_sc, xn_sc, acc_sc, carry_sc) = refs[1 + 2 * n_mix:]
    i = pl.program_id(1)
    j = pl.program_id(2)
    tm = h_ref.shape[0]

    @pl.when(j == 0)
    def _():
        y = h_ref[...]
        for a_ref, wo_ref in zip(a_refs, wo_refs):
            y = y + _dot(a_ref[...].astype(BF16), wo_ref[...])
        h1_sc[...] = y
        xn_sc[...] = _rms(y, nf_ref[...]).astype(BF16)
        acc_sc[...] = jnp.zeros(acc_sc.shape, F32)

    @pl.when(i == 0)
    def _():
        carry_sc[j] = cst_ref[0]

    xn = xn_sc[...]
    tf = wg_ref.shape[1]
    chunk = min(tf, MXU_WIDTH)
    contrib = None
    for c in range(tf // chunk):
        cols = slice(c * chunk, (c + 1) * chunk)
        g = _dot(xn, wg_ref[:, cols])
        u = _dot(xn, wu_ref[:, cols])
        xp = jnp.concatenate([carry_sc[j, :, cols], g], axis=0)
        gc = cb_ref[:, cols]
        for kk in range(FFN_CONV):
            off = pad - (FFN_CONV - 1 - kk) * stride
            gc = gc + xp[off:off + tm] * cw_ref[kk:kk + 1, cols]
        carry_sc[j, :, cols] = xp[tm:tm + pad]
        part = _dot((_gelu(gc) * u).astype(BF16), wout_ref[cols, :])
        contrib = part if contrib is None else contrib + part
    acc_sc[...] += contrib

    @pl.when(i == pl.num_programs(1) - 1)
    def _():
        fc_ref[0, j] = carry_sc[j]

    @pl.when(j == pl.num_programs(2) - 1)
    def _():
        h2 = h1_sc[...] + acc_sc[...]
        gate = jax.nn.sigmoid(_dot(_rms(h2, pgn_ref[...]).astype(BF16), wpg_ref[...]))
        e = _dot(p_ref[0].astype(BF16), wpp_ref[...])
        o_ref[...] = h2 + _rms(e, ppn_ref[...]) * gate


def _post_mixer(h, mix, wo_bf16, p, nf, wffn_in_bf16, wffn_out_bf16, cw, cb, cst, pgn, wpg_bf16, wpp_bf16,
                ppn, *, p_layer, n_seq, stride, tm, tf):
    n, d = h.shape
    dff = wffn_out_bf16.shape[0]
    nj = dff // tf
    pad = cst.shape[1]
    ni = n // n_seq // tm
    n_mix = len(mix)
    row = lambda width: pl.BlockSpec((tm, width), lambda b, i, j: (b * ni + i, 0))
    full = lambda shape: pl.BlockSpec(shape, lambda b, i, j: (0, 0))
    in_specs = (
        [row(d)] + [row(a.shape[1]) for a in mix] + [full(w_.shape) for w_ in wo_bf16]
        + [
            pl.BlockSpec((1, tm, p.shape[2]), lambda b, i, j: (p_layer, b * ni + i, 0)),
            full((1, d)),
            pl.BlockSpec((d, tf), lambda b, i, j: (0, j)),
            pl.BlockSpec((d, tf), lambda b, i, j: (0, nj + j)),
            pl.BlockSpec((tf, d), lambda b, i, j: (j, 0)),
            pl.BlockSpec((FFN_CONV, tf), lambda b, i, j: (0, j)),
            pl.BlockSpec((1, tf), lambda b, i, j: (0, j)),
            pl.BlockSpec((1, pad, tf), lambda b, i, j: (b, 0, j)),
            full((1, d)),
            full(wpg_bf16.shape),
            full(wpp_bf16.shape),
            full((1, d)),
        ])
    out, fc = pl.pallas_call(
        functools.partial(_post_kernel, n_mix=n_mix, stride=stride, pad=pad),
        grid=(n_seq, ni, nj),
        in_specs=in_specs,
        out_specs=[
            pl.BlockSpec((tm, d), lambda b, i, j: (b * ni + i, 0)),
            pl.BlockSpec((1, nj, pad, tf), lambda b, i, j: (b, 0, 0, 0)),
        ],
        out_shape=[
            jax.ShapeDtypeStruct((n, d), F32),
            jax.ShapeDtypeStruct((n_seq, nj, pad, tf), F32),
        ],
        scratch_shapes=[
            pltpu.VMEM((tm, d), F32),
            pltpu.VMEM((tm, d), BF16),
            pltpu.VMEM((tm, d), F32),
            pltpu.VMEM((nj, pad, tf), F32),
        ],
        compiler_params=_params("arbitrary", "arbitrary", "arbitrary"),
        name="post_mixer",
    )(h, *mix, *wo_bf16, p, nf.reshape(1, d), wffn_in_bf16, wffn_in_bf16, wffn_out_bf16, cw,
      cb.reshape(1, dff), cst, pgn.reshape(1, d), wpg_bf16, wpp_bf16, ppn.reshape(1, d))
    return out, jnp.transpose(fc, (0, 2, 1, 3)).reshape(n_seq, pad, dff)


def _t5_bucket(dist):
    n = jnp.maximum(dist, 0)
    max_exact = REL_BUCKETS // 2
    nf = jnp.maximum(n, max_exact).astype(F32)
    large = max_exact + (jnp.log(nf / max_exact) / math.log(REL_MAX_DIST / max_exact)
                         * (REL_BUCKETS - max_exact)).astype(jnp.int32)
    large = jnp.minimum(large, REL_BUCKETS - 1)
    return jnp.where(n < max_exact, n, large)


def _bias_table(rel_bias, dist):
    onehot = (_t5_bucket(jnp.asarray(dist))[..., None] == jnp.arange(REL_BUCKETS)).astype(F32)
    return jnp.einsum('...b,bh->h...', onehot, rel_bias.astype(F32), precision=lax.Precision.HIGHEST)


def _top3(sc, idxf, n, axis=-1):
    picks = []
    for _ in range(MOBA_TOPK):
        m = jnp.max(sc, axis=axis, keepdims=True)
        first = jnp.min(jnp.where(sc == m, idxf, float(n)), axis=axis, keepdims=True)
        picks.append((first, m))
        sc = jnp.where(idxf == first, -jnp.inf, sc)
    return picks


def _moba_prompt_kernel(pt_ref, q_ref, qn_ref, k_ref, vt_ref, bo_ref, bp_ref, bf_ref, *rest, pages_per_step):
    if pages_per_step:
        ck_ref, o_ref, kmo_ref, km_sc, mask_sc, s0_sc, s1_sc, pbuf, psem = rest
    else:
        o_ref, km_sc, mask_sc, s0_sc, s1_sc = rest
    i = pl.program_id(2)
    nb = km_sc.shape[0]
    blk = MOBA_BLOCK
    hd = C_HEAD_DIM
    ext = hd - MASK_ROWS

    if pages_per_step:
        step = (pl.program_id(0) * pl.num_programs(1) + pl.program_id(1)) * pl.num_programs(2) + i
        steps_per_seq = pt_ref.shape[1] // pages_per_step
        seq = step // steps_per_seq
        chunk = step % steps_per_seq

        def page_copy(kk):
            return pltpu.make_async_copy(ck_ref.at[pt_ref[seq, chunk * pages_per_step + kk]], pbuf.at[kk],
                                         psem.at[kk])

        for kk in range(pages_per_step):
            page_copy(kk).start()

    @pl.when(i == 0)
    def _():
        def body(jb, c):
            kb = k_ref[0, pl.ds(pl.multiple_of(jb * blk, blk), blk), :]
            km_sc[pl.ds(jb, 1), :] = jnp.sum(kb, axis=0, keepdims=True) * (1.0 / blk)
            return c
        lax.fori_loop(0, nb, body, 0)
        r = lax.broadcasted_iota(jnp.int32, (2, MASK_ROWS, blk), 1)
        mask_sc[...] = jnp.where(r < nb, NEG, 0.0)

    def mask_rows(kmh, q_t, tile):
        rowf = lax.broadcasted_iota(jnp.int32, (nb, blk), 0).astype(F32)
        sc = jnp.dot(kmh, q_t, precision=lax.Precision.HIGHEST, preferred_element_type=F32)
        sc = jnp.where(rowf < tile.astype(F32), sc, -jnp.inf)
        sel = jnp.zeros((nb, blk), F32)
        for first, m in _top3(sc, rowf, nb, axis=0):
            sel = jnp.where((rowf == first) & (m > -jnp.inf), 1.0, sel)
        mask = jnp.where(sel > 0.0, 0.0, NEG)
        if nb < MASK_ROWS:
            mask = jnp.concatenate([mask, jnp.zeros((MASK_ROWS - nb, blk), F32)], axis=0)
        return mask

    def bias_rows(bias):
        hi = bias.astype(BF16).astype(F32)
        r = lax.broadcasted_iota(jnp.int32, (ext, blk), 0)
        return jnp.where(r == 0, hi, jnp.where(r == 1, bias - hi, 0.0))

    q_t = q_ref[0].T
    c = hd ** -0.5 * LOG2E
    qp_a = jnp.concatenate([q_t[:hd] * c, mask_sc[0], bias_rows(bf_ref[0])], axis=0).astype(BF16)
    qp_b = jnp.concatenate([mask_sc[1], bias_rows(bf_ref[1]), q_t[hd:] * c], axis=0).astype(BF16)

    def logits(jb, n_blk, block_ids, far):
        start = pl.multiple_of(jb * blk, blk)
        lane = lax.broadcasted_iota(jnp.int32, (1, LANES), 1)
        lo = lax.broadcasted_iota(jnp.int32, (blk, LANES), 1) < hd
        kbs_a, kbs_b = [], []
        for g in range(n_blk):
            kb = k_ref[0, pl.ds(pl.multiple_of(start + g * blk, blk), blk), :]
            e_a = jnp.zeros((1, LANES), F32)
            e_b = jnp.zeros((1, LANES), F32)
            if block_ids is not None:
                e_a = jnp.where(lane == hd + block_ids[g], 1.0, e_a)
                e_b = jnp.where(lane == block_ids[g], 1.0, e_b)
            if far:
                e_a = jnp.where((lane == hd + MASK_ROWS) | (lane == hd + MASK_ROWS + 1), 1.0, e_a)
                e_b = jnp.where((lane == MASK_ROWS) | (lane == MASK_ROWS + 1), 1.0, e_b)
            kbs_a.append(jnp.where(lo, kb, e_a).astype(BF16))
            kbs_b.append(jnp.where(lo, e_b, kb).astype(BF16))
        kb_a = jnp.concatenate(kbs_a, axis=0) if n_blk > 1 else kbs_a[0]
        kb_b = jnp.concatenate(kbs_b, axis=0) if n_blk > 1 else kbs_b[0]
        return _dot(kb_a, qp_a), _dot(kb_b, qp_b)

    def fold(carry, s_a, s_b, jb, n_blk):
        m_a, acc_a, m_b, acc_b = carry
        keys = n_blk * blk
        vt = vt_ref[0, :, pl.ds(pl.multiple_of(jb * blk, blk), keys)]
        ones = jnp.ones((PV_ONES, keys), F32)
        vt_a = jnp.concatenate([vt[:hd], ones], axis=0).astype(BF16)
        vt_b = jnp.concatenate([vt[hd:], ones], axis=0).astype(BF16)
        mn_a = jnp.maximum(m_a, jnp.max(s_a, axis=0, keepdims=True))
        mn_b = jnp.maximum(m_b, jnp.max(s_b, axis=0, keepdims=True))
        p_a = jnp.exp2(s_a - mn_a).astype(BF16)
        p_b = jnp.exp2(s_b - mn_b).astype(BF16)
        acc_a = jnp.exp2(m_a - mn_a) * acc_a + _dot(vt_a, p_a)
        acc_b = jnp.exp2(m_b - mn_b) * acc_b + _dot(vt_b, p_b)
        return mn_a, acc_a, mn_b, acc_b

    qn_t = qn_ref[0].T
    km = km_sc[...]
    km_lo = lax.broadcasted_iota(jnp.int32, (nb, LANES), 1) < hd
    mask_next_a = mask_rows(jnp.where(km_lo, km, 0.0), qn_t, i + 1)
    mask_next_b = mask_rows(jnp.where(km_lo, 0.0, km), qn_t, i + 1)

    m0 = jnp.full((1, blk), NEG, F32)
    acc0 = jnp.zeros((hd + PV_ONES, blk), F32)
    carry = (m0, acc0, m0, acc0)
    no_mask = -LANES
    first = i == 0
    near = jnp.maximum(i - 1, 0)
    s_a, s_b = logits(near, GROUP, [jnp.where(first, no_mask, i - 1), jnp.where(first, nb - 1, no_mask)], False)
    s0_sc[0] = s_a + jnp.concatenate([jnp.where(first, bo_ref[0], bp_ref[0]), bo_ref[0]], axis=0)
    s0_sc[1] = s_b + jnp.concatenate([jnp.where(first, bo_ref[1], bp_ref[1]), bo_ref[1]], axis=0)

    n_far = jnp.maximum(i - 1, 0)
    n_grp = (n_far + GROUP - 1) // GROUP

    def far_logits(grp, s_sc):
        jb = jnp.minimum(grp * GROUP, nb - GROUP)
        ids = [jnp.where(grp * GROUP + g < n_far, grp * GROUP + g, nb - 1) for g in range(GROUP)]
        s_sc[0], s_sc[1] = logits(jb, GROUP, ids, True)

    def far_fold(cy, grp, s_sc):
        return fold(cy, s_sc[0], s_sc[1], jnp.minimum(grp * GROUP, nb - GROUP), GROUP)

    far_logits(0, s1_sc)
    carry = fold(carry, s0_sc[0], s0_sc[1], near, GROUP)
    mask_sc[0] = mask_next_a
    mask_sc[1] = mask_next_b

    def far_pair(it, cy):
        far_logits(2 * it + 1, s0_sc)
        cy = far_fold(cy, 2 * it, s1_sc)
        far_logits(2 * it + 2, s1_sc)
        return far_fold(cy, 2 * it + 1, s0_sc)

    m_a, acc_a, m_b, acc_b = lax.fori_loop(0, (n_grp + 1) // 2, far_pair, carry)
    out_t = jnp.concatenate([acc_a[:hd] / acc_a[hd:hd + 1], acc_b[:hd] / acc_b[hd:hd + 1]], axis=0)
    o_ref[0] = out_t.T

    if pages_per_step:
        @pl.when(chunk == 0)
        def _():
            kmo_ref[...] = jnp.zeros(kmo_ref.shape, F32)

        for kk in range(pages_per_step):
            page_copy(kk).wait()
        means = kmo_ref[0]
        lane = lax.broadcasted_iota(jnp.int32, means.shape, 2)
        blocks_per_step = pages_per_step // PAGES_PER_BLOCK
        for g in range(blocks_per_step):
            tot = pbuf[g * PAGES_PER_BLOCK]
            for kk in range(1, PAGES_PER_BLOCK):
                tot = tot + pbuf[g * PAGES_PER_BLOCK + kk]
            col = jnp.sum(tot, axis=-1, keepdims=True) * (1.0 / MOBA_BLOCK)
            means = jnp.where(lane == chunk * blocks_per_step + g, col, means)
        kmo_ref[0] = means


def _moba_prompt(q, k, v_t, rel_bias, paged=None):
    b, t, width = q.shape
    blk = MOBA_BLOCK
    nb = t // blk
    assert nb <= MASK_ROWS
    hp = width // LANES
    pages_per_step = 0
    page_table = jnp.zeros((1, 1), jnp.int32)
    if paged is not None:
        ck_t, pt = paged
        n_seq2, n_pages = pt.shape
        pps, rem = divmod(n_seq2 * n_pages, b * hp * nb)
        if rem == 0 and pps > 0 and pps % PAGES_PER_BLOCK == 0 and n_pages % pps == 0:
            pages_per_step, page_table = pps, pt
    ll = np.arange(blk)[:, None]
    ii = np.arange(blk)[None, :]
    bias_own = jnp.where(ii >= ll, _bias_table(rel_bias, np.maximum(ii - ll, 0)) * LOG2E, NEG)
    bias_prev = _bias_table(rel_bias, blk + ii - ll) * LOG2E
    bias_far = (_bias_table(rel_bias, np.array([2 * blk])) * LOG2E).reshape(-1, 1, 1)
    pair = lambda shape: pl.BlockSpec((2,) + shape, lambda bi, hi, qi, pt: (hi, 0, 0))
    in_specs = [
        pl.BlockSpec((1, blk, LANES), lambda bi, hi, qi, pt: (bi, qi, hi)),
        pl.BlockSpec((1, blk, LANES), lambda bi, hi, qi, pt: (bi, jnp.minimum(qi + 1, nb - 1), hi)),
        pl.BlockSpec((1, t, LANES), lambda bi, hi, qi, pt: (bi, 0, hi)),
        pl.BlockSpec((1, LANES, t), lambda bi, hi, qi, pt: (bi, hi, 0)),
        pair((blk, blk)), pair((blk, blk)), pair((1, 1)),
    ]
    out_specs = [pl.BlockSpec((1, blk, LANES), lambda bi, hi, qi, pt: (bi, qi, hi))]
    out_shape = [jax.ShapeDtypeStruct((b, t, width), F32)]
    scratch_shapes = [pltpu.VMEM((nb, LANES), F32),
                      pltpu.VMEM((2, MASK_ROWS, blk), F32),
                      pltpu.VMEM((2, GROUP * blk, blk), F32),
                      pltpu.VMEM((2, GROUP * blk, blk), F32)]
    args = [page_table, q, q, k, v_t, bias_own, bias_prev, bias_far]
    if pages_per_step:
        _, h2, dh2, page = ck_t.shape
        steps_per_seq = n_pages // pages_per_step
        in_specs.append(pl.BlockSpec(memory_space=pl.ANY))
        out_specs.append(pl.BlockSpec((1, h2, dh2, n_pages // PAGES_PER_BLOCK),
                                      lambda bi, hi, qi, pt: (((bi * hp + hi) * nb + qi) // steps_per_seq, 0, 0, 0)))
        out_shape.append(jax.ShapeDtypeStruct((n_seq2, h2, dh2, n_pages // PAGES_PER_BLOCK), F32))
        scratch_shapes += [pltpu.VMEM((pages_per_step, h2, dh2, page), F32),
                           pltpu.SemaphoreType.DMA((pages_per_step,))]
        args.append(ck_t)
    outs = pl.pallas_call(
        functools.partial(_moba_prompt_kernel, pages_per_step=pages_per_step),
        grid_spec=pltpu.PrefetchScalarGridSpec(
            num_scalar_prefetch=1, grid=(b, hp, nb), in_specs=in_specs, out_specs=out_specs,
            scratch_shapes=scratch_shapes),
        out_shape=out_shape,
        compiler_params=_params("arbitrary", "arbitrary", "arbitrary"),
        name="moba_prompt",
    )(*args)
    return (outs[0], outs[1]) if pages_per_step else (outs[0], None)


def _kmean_kernel(pt_ref, *refs):
    page_refs, o_ref = refs[:-1], refs[-1]
    j = pl.program_id(1)

    @pl.when(j == 0)
    def _():
        o_ref[...] = jnp.zeros(o_ref.shape, F32)

    lane = lax.broadcasted_iota(jnp.int32, o_ref.shape[1:], 2)
    out = o_ref[0]
    for g in range(len(page_refs) // PAGES_PER_BLOCK):
        pages = page_refs[g * PAGES_PER_BLOCK:(g + 1) * PAGES_PER_BLOCK]
        tot = pages[0][0]
        for pr in pages[1:]:
            tot = tot + pr[0]
        col = jnp.sum(tot, axis=-1, keepdims=True) * (1.0 / MOBA_BLOCK)
        out = jnp.where(lane == j * KMEAN_BLOCKS_PER_STEP + g, col, out)
    o_ref[0] = out


def _block_means(ck_t, page_table, nb):
    _, h, dh, page = ck_t.shape
    b = page_table.shape[0]
    g = KMEAN_BLOCKS_PER_STEP
    assert nb % g == 0
    n_pages = g * PAGES_PER_BLOCK
    page_spec = lambda k: pl.BlockSpec((1, h, dh, page), lambda bi, j, pt: (pt[bi, n_pages * j + k], 0, 0, 0))
    return pl.pallas_call(
        _kmean_kernel,
        grid_spec=pltpu.PrefetchScalarGridSpec(
            num_scalar_prefetch=1,
            grid=(b, nb // g),
            in_specs=[page_spec(k) for k in range(n_pages)],
            out_specs=pl.BlockSpec((1, h, dh, nb), lambda bi, j, pt: (bi, 0, 0, 0)),
        ),
        out_shape=jax.ShapeDtypeStruct((b, h, dh, nb), F32),
        compiler_params=_params("arbitrary", "arbitrary"),
        name="moba_block_means",
    )(page_table, *([ck_t] * n_pages))


def _select_kernel(qt_ref, km_ref, o_ref):
    s_len, _, h = qt_ref.shape[1:]
    nb = km_ref.shape[3]
    rows = s_len * h
    rid = lax.broadcasted_iota(jnp.int32, (rows, nb), 0)
    sc = jnp.zeros((rows, nb), F32)
    for s in range(s_len):
        qt = qt_ref[0, s]
        for hh in range(h):
            sc = jnp.where(rid == s * h + hh, jnp.sum(km_ref[0, hh] * qt[:, hh:hh + 1], axis=0, keepdims=True), sc)
    colf = lax.broadcasted_iota(jnp.int32, (rows, nb), 1).astype(F32)
    picks = [jnp.minimum(first, nb - 1.0).astype(jnp.int32) for first, _ in _top3(sc, colf, nb)]
    lane = lax.broadcasted_iota(jnp.int32, (rows, LANES), 1)
    o_ref[0] = jnp.where(lane == 0, picks[0], jnp.where(lane == 1, picks[1], picks[2]))


def _select_blocks(q_t, kmean_t):
    b, s_len, dh, h = q_t.shape
    nb = kmean_t.shape[3]
    out = pl.pallas_call(
        _select_kernel,
        grid=(b,),
        in_specs=[
            pl.BlockSpec((1, s_len, dh, h), lambda bi: (bi, 0, 0, 0)),
            pl.BlockSpec((1, h, dh, nb), lambda bi: (bi, 0, 0, 0)),
        ],
        out_specs=pl.BlockSpec((1, s_len * h, LANES), lambda bi: (bi, 0, 0)),
        out_shape=jax.ShapeDtypeStruct((b, s_len * h, LANES), jnp.int32),
        compiler_params=_params("arbitrary"),
        name="moba_select",
    )(q_t, kmean_t)
    return out[:, :, :MOBA_TOPK].reshape(b, s_len, h, MOBA_TOPK)


def _moba_sample_kernel(pt_ref, sel_ref, ck_ref, cv_ref, qt_ref, kn_ref, vn_ref, bl_ref, bo_ref, bf_ref,
                        o_ref, kbuf, vbuf, sem, *, nb):
    b = pl.program_id(0)
    s = pl.program_id(1)
    s_len = pl.num_programs(1)
    _, h, _, dh, _ = kbuf.shape
    ppb = PAGES_PER_BLOCK
    step = b * s_len + s
    cur = step % 2

    def copies(st, buf):
        out = []
        for hh in range(h):
            for n in range(MOBA_TOPK):
                idx = sel_ref[(st * h + hh) * MOBA_TOPK + n]
                for pg in range(ppb):
                    phys = pt_ref[st // s_len, idx * ppb + pg]
                    slot = n * ppb + pg
                    out.append(pltpu.make_async_copy(ck_ref.at[phys, hh], kbuf.at[buf, hh, slot], sem.at[buf, 0]))
                    out.append(pltpu.make_async_copy(cv_ref.at[phys, hh], vbuf.at[buf, hh, slot], sem.at[buf, 1]))
        return out

    @pl.when(step == 0)
    def _():
        for cp in copies(step, cur):
            cp.start()

    @pl.when(step + 1 < pl.num_programs(0) * s_len)
    def _():
        for cp in copies(step + 1, 1 - cur):
            cp.start()

    for cp in copies(step, cur):
        cp.wait()

    kbuf = kbuf.at[cur]
    vbuf = vbuf.at[cur]
    lane = lax.broadcasted_iota(jnp.int32, (dh, LANES), 1)
    qt = qt_ref[0, 0] * (dh ** -0.5)
    out = jnp.zeros((dh, LANES), F32)
    for hh in range(h):
        qc = qt[:, hh:hh + 1]
        far = bf_ref[hh:hh + 1, :]
        logits = []
        for n in range(MOBA_TOPK):
            idx = sel_ref[((b * s_len + s) * h + hh) * MOBA_TOPK + n]
            for pg in range(ppb):
                lg = jnp.sum(kbuf[hh, n * ppb + pg] * qc, axis=0, keepdims=True)
                logits.append(lg + jnp.where(idx == nb - 1, bl_ref[0, hh, pg:pg + 1, :], far))
        lown = jnp.sum(kn_ref[0, hh] * qc, axis=0, keepdims=True) + bo_ref[0, hh:hh + 1, :]
        m = jnp.max(lown, axis=-1, keepdims=True)
        for lg in logits:
            m = jnp.maximum(m, jnp.max(lg, axis=-1, keepdims=True))
        pown = jnp.exp(lown - m)
        den = jnp.sum(pown, axis=-1, keepdims=True)
        col = jnp.sum(vn_ref[0, hh] * pown, axis=-1, keepdims=True)
        psum = None
        pv = None
        for slot, lg in enumerate(logits):
            p = jnp.exp(lg - m)
            psum = p if psum is None else psum + p
            pv = vbuf[hh, slot] * p if pv is None else pv + vbuf[hh, slot] * p
        den = den + jnp.sum(psum, axis=-1, keepdims=True)
        col = col + jnp.sum(pv, axis=-1, keepdims=True)
        out = jnp.where(lane == hh, col / den, out)
    o_ref[0, 0] = out


def _moba_sample(q, k_new, v_new, ck_t, cv_t, page_table, rel_bias, past_len, kmean_t=None):
    b, s_len, h, dh = q.shape
    page = ck_t.shape[3]
    blk = MOBA_BLOCK
    assert past_len % blk == 0 and blk == PAGES_PER_BLOCK * page
    nb = past_len // blk
    assert nb >= MOBA_TOPK
    q_t = jnp.transpose(q, (0, 1, 3, 2))
    if kmean_t is None:
        kmean_t = _block_means(ck_t, page_table, nb)
    sel = _select_blocks(q_t, kmean_t)

    ss = np.arange(s_len)
    bias_last = _bias_table(rel_bias, blk + ss[:, None] - np.arange(blk)[None, :])
    bias_last = jnp.transpose(bias_last, (1, 0, 2)).reshape(s_len, h, PAGES_PER_BLOCK, page)
    own_d = ss[:, None] - ss[None, :]
    bias_own = jnp.where(own_d >= 0, _bias_table(rel_bias, np.maximum(own_d, 0)), NEG)
    bias_own = jnp.transpose(bias_own, (1, 0, 2))
    bias_far = _bias_table(rel_bias, np.array([2 * blk]))
    kn_t = jnp.transpose(k_new, (0, 2, 3, 1))
    vn_t = jnp.transpose(v_new, (0, 2, 3, 1))

    out = pl.pallas_call(
        functools.partial(_moba_sample_kernel, nb=nb),
        grid_spec=pltpu.PrefetchScalarGridSpec(
            num_scalar_prefetch=2,
            grid=(b, s_len),
            in_specs=[
                pl.BlockSpec(memory_space=pl.ANY),
                pl.BlockSpec(memory_space=pl.ANY),
                pl.BlockSpec((1, 1, dh, h), lambda bi, si, pt, sl: (bi, si, 0, 0)),
                pl.BlockSpec((1, h, dh, s_len), lambda bi, si, pt, sl: (bi, 0, 0, 0)),
                pl.BlockSpec((1, h, dh, s_len), lambda bi, si, pt, sl: (bi, 0, 0, 0)),
                pl.BlockSpec((1, h, PAGES_PER_BLOCK, page), lambda bi, si, pt, sl: (si, 0, 0, 0)),
                pl.BlockSpec((1, h, s_len), lambda bi, si, pt, sl: (si, 0, 0)),
                pl.BlockSpec((h, 1), lambda bi, si, pt, sl: (0, 0)),
            ],
            out_specs=pl.BlockSpec((1, 1, dh, LANES), lambda bi, si, pt, sl: (bi, si, 0, 0)),
            scratch_shapes=[
                pltpu.VMEM((2, h, PAGES_PER_BLOCK * MOBA_TOPK, dh, page), F32),
                pltpu.VMEM((2, h, PAGES_PER_BLOCK * MOBA_TOPK, dh, page), F32),
                pltpu.SemaphoreType.DMA((2, 2)),
            ],
        ),
        out_shape=jax.ShapeDtypeStruct((b, s_len, dh, LANES), F32),
        compiler_params=_params("arbitrary", "arbitrary"),
        name="moba_sample",
    )(page_table, sel.reshape(-1), ck_t, cv_t, q_t, kn_t, vn_t, bias_last, bias_own, bias_far)
    return jnp.transpose(out[:, :, :, :h], (0, 1, 3, 2))


def _prep_weights(w):
    bf = lambda a: a.astype(BF16)
    n_even = w['w0_in'].shape[0]
    n_odd = w['c_in'].shape[0]
    out = dict(w)
    out['w0_in'] = bf(w['w0_in'])
    out['w0_out'] = bf(w['w0_out'])
    out['lru_wg'] = [bf(jnp.concatenate([_block_diag(w['lru_wa'][e]), _block_diag(w['lru_wi'][e])], axis=1))
                     for e in range(n_even)]
    out['c_in'] = bf(w['c_in'])
    out['c_out'] = bf(w['c_out'])
    out['c_gain'] = [jnp.concatenate([jnp.tile(w['c_qn'][o], C_HEADS), jnp.tile(w['c_kn'][o], C_HEADS),
                                      jnp.ones((C_HEADS * C_HEAD_DIM,), F32)]).reshape(1, -1)
                     for o in range(n_odd)]
    out['ffn_in'] = bf(w['ffn_in'])
    out['ffn_out'] = bf(w['ffn_out'])
    out['ple_proj'] = bf(w['ple_proj'])
    out['ple_gate'] = bf(w['ple_gate'])
    return out


def _post(w, l, h, mix, wo, p, p_layer, cst, *, n_seq, stride, tm, tf):
    return _post_mixer(h, mix, wo, p, w['norm_ffn'][l], w['ffn_in'][l], w['ffn_out'][l], w['ffn_conv_w'][l],
                       w['ffn_conv_b'][l], cst, w['ple_gate_norm'][l], w['ple_gate'][l], w['ple_proj'][l],
                       w['ple_post_norm'][l], p_layer=p_layer, n_seq=n_seq, stride=stride, tm=tm, tf=tf)


def _prompt_trunk(x, p, w, rel_bias, ck_ts, page_table):
    b, t, d = x.shape
    n = b * t
    depth = p.shape[0]
    h = x.reshape(n, d)
    tm = 512
    rets, lrus, lconvs, fconvs, ks, vs, kmeans = [], [], [], [], [], [], []
    lw = w['lru_conv_w'].shape[2]
    dff = w['ffn_out'].shape[1]
    for l in range(depth):
        cst = jnp.zeros((b, SUBLANES, dff), F32)
        if l % 2 == 0:
            e = l // 2
            proj = _norm_proj(h, w['norm_mix'][l], w['w0_in'][e], tm=512, tn=1024).reshape(b, t, -1)
            o, s_new = _retention(proj, jnp.zeros((b, RET_HEADS, RET_DIM, RET_DIM), F32), jnp.arange(t),
                                  w['ret_gn_g'][e], w['ret_gn_b'][e], c_eff=RET_CHUNK, c_pad=RET_CHUNK,
                                  rows_per_step=512)
            ob, h_last, tail = _lru(proj, jnp.zeros((b, SUBLANES, lw), F32), jnp.zeros((b, 1, lw), F32),
                                    w['lru_conv_w'][e], w['lru_conv_b'][e], w['lru_wg'][e], w['lru_ba'][e],
                                    w['lru_bi'][e], w['lru_lam'][e], stride=1, rows_per_step=512)
            rets.append(s_new)
            lrus.append(h_last[:, 0])
            lconvs.append(tail[:, SUBLANES - (LRU_CONV - 1):])
            rw = o.shape[-1]
            mix = [o.reshape(n, rw), ob.reshape(n, lw)]
            wo = [w['w0_out'][e][:rw], w['w0_out'][e][rw:]]
        else:
            o_ = l // 2
            q, k, k_t, v_t = _norm_proj(h, w['norm_mix'][l], w['c_in'][o_], tm=512, tn=1024,
                                        head_gain=w['c_gain'][o_], n_rms_tiles=2,
                                        plan=((False,), (False, True), (True,)), n_seq=b)
            cw = q.shape[1]
            att, kmean_t = _moba_prompt(q.reshape(b, t, cw), k.reshape(b, t, cw), v_t, rel_bias,
                                        paged=(ck_ts[o_], page_table))
            kmeans.append(kmean_t)
            heads = lambda a: jnp.transpose(a.reshape(b, C_HEADS, C_HEAD_DIM, t), (0, 3, 1, 2))
            ks.append(heads(k_t))
            vs.append(heads(v_t))
            mix = [att.reshape(n, cw)]
            wo = [w['c_out'][o_]]
        h, fc = _post(w, l, h, mix, wo, p.reshape(depth, n, -1), l, cst, n_seq=b, stride=1, tm=tm, tf=512)
        fconvs.append(fc[:, SUBLANES - (FFN_CONV - 1):])
    return (h.reshape(b, t, d), jnp.stack(rets), jnp.stack(lrus), jnp.stack(lconvs), jnp.stack(fconvs),
            jnp.stack(ks), jnp.stack(vs), kmeans)


def _sample_trunk(x, p, ret0, lru0, lconv0, fconv0, ck_ts, cv_ts, page_table, w, rel_bias, kmeans):
    b, t, d = x.shape
    n = b * t
    depth = p.shape[0]
    past_len = page_table.shape[1] * ck_ts[0].shape[3]
    tmaj = lambda a: jnp.transpose(a, (1, 0) + tuple(range(2, a.ndim)))
    h = tmaj(x).reshape(n, d)
    rets, lrus, lconvs, fconvs, ks, vs = [], [], [], [], [], []
    lw = w['lru_conv_w'].shape[2]
    for l in range(depth):
        cst = tmaj(fconv0[l]).reshape(1, (FFN_CONV - 1) * b, -1)
        if l % 2 == 0:
            e = l // 2
            proj = _norm_proj(h, w['norm_mix'][l], w['w0_in'][e], tm=n, tn=1024)
            rcols = 4 * RET_HEADS * RET_DIM
            pr = tmaj(proj[:, :rcols].reshape(t, b, rcols))
            pr = jnp.pad(pr, ((0, 0), (0, SUBLANES - t), (0, 0)))
            pos = past_len + jnp.arange(SUBLANES)
            o, s_new = _retention(pr, ret0[e].astype(F32), pos, w['ret_gn_g'][e], w['ret_gn_b'][e],
                                  c_eff=t, c_pad=SUBLANES, rows_per_step=SUBLANES)
            o = tmaj(o[:, :t]).reshape(n, -1)
            tail0 = tmaj(lconv0[e]).reshape(1, (LRU_CONV - 1) * b, lw)
            ob, h_last, tail = _lru(proj.reshape(1, n, -1), tail0, lru0[e].reshape(1, b, lw),
                                    w['lru_conv_w'][e], w['lru_conv_b'][e], w['lru_wg'][e], w['lru_ba'][e],
                                    w['lru_bi'][e], w['lru_lam'][e], stride=b, rows_per_step=n)
            rets.append(s_new)
            lrus.append(h_last[0])
            lconvs.append(tmaj(tail.reshape(LRU_CONV - 1, b, lw)))
            rw = o.shape[-1]
            mix = [o, ob.reshape(n, lw)]
            wo = [w['w0_out'][e][:rw], w['w0_out'][e][rw:]]
        else:
            o_ = l // 2
            q, k, v = _norm_proj(h, w['norm_mix'][l], w['c_in'][o_], tm=n, tn=1024, head_gain=w['c_gain'][o_],
                                 n_rms_tiles=2, plan=((False,), (False,), (False,)))
            heads = lambda a: tmaj(a.reshape(t, b, C_HEADS, C_HEAD_DIM))
            q, k, v = heads(q), heads(k), heads(v)
            att = _moba_sample(q, k, v, ck_ts[o_], cv_ts[o_], page_table, rel_bias, past_len, kmean_t=kmeans[o_])
            ks.append(k)
            vs.append(v)
            mix = [tmaj(att).reshape(n, -1)]
            wo = [w['c_out'][o_]]
        h, fc = _post(w, l, h, mix, wo, tmaj(p[l]).reshape(1, n, -1), 0, cst, n_seq=1, stride=b, tm=n, tf=512)
        fconvs.append(tmaj(fc.reshape(FFN_CONV - 1, b, -1)))
    return (tmaj(h.reshape(t, b, d)), jnp.stack(rets), jnp.stack(lrus), jnp.stack(lconvs), jnp.stack(fconvs),
            jnp.stack(ks), jnp.stack(vs))


def kernel(x_prompt, x_sample, state_ret, state_lru, state_lru_conv, state_ffn_conv, cache_k, cache_v, page_table, p_prompt, p_sample, norm_mix, norm_ffn, w0_in, ret_gn_g, ret_gn_b, lru_conv_w, lru_conv_b, lru_wa, lru_ba, lru_wi, lru_bi, lru_lam, w0_out, c_in, c_qn, c_kn, c_out, rel_bias, ffn_in, ffn_conv_w, ffn_conv_b, ffn_out, ple_proj, ple_gate_norm, ple_gate, ple_post_norm):
    w = _prep_weights(dict(
        norm_mix=norm_mix, norm_ffn=norm_ffn, w0_in=w0_in, ret_gn_g=ret_gn_g, ret_gn_b=ret_gn_b,
        lru_conv_w=lru_conv_w, lru_conv_b=lru_conv_b, lru_wa=lru_wa, lru_ba=lru_ba, lru_wi=lru_wi,
        lru_bi=lru_bi, lru_lam=lru_lam, w0_out=w0_out, c_in=c_in, c_qn=c_qn, c_kn=c_kn, c_out=c_out,
        ffn_in=ffn_in, ffn_conv_w=ffn_conv_w, ffn_conv_b=ffn_conv_b, ffn_out=ffn_out, ple_proj=ple_proj,
        ple_gate_norm=ple_gate_norm, ple_gate=ple_gate, ple_post_norm=ple_post_norm))
    ck_ts = [jnp.transpose(cache_k[o], (0, 2, 3, 1)) for o in range(cache_k.shape[0])]
    cv_ts = [jnp.transpose(cache_v[o], (0, 2, 3, 1)) for o in range(cache_v.shape[0])]
    y_p, ret_p, lru_p, lconv_p, fconv_p, k_p, v_p, kmeans = _prompt_trunk(
        x_prompt, p_prompt, w, rel_bias, ck_ts, page_table)
    y_s, ret_s, lru_s, lconv_s, fconv_s, k_s, v_s = _sample_trunk(
        x_sample, p_sample, state_ret, state_lru, state_lru_conv, state_ffn_conv, ck_ts, cv_ts, page_table,
        w, rel_bias, kmeans)
    return (y_p, y_s, ret_p, ret_s, lru_p, lru_s, lconv_p, lconv_s, fconv_p, fconv_s, k_p, k_s, v_p, v_s)
```

```python
import functools
import math

import numpy as np
import jax
import jax.numpy as jnp
from jax import lax
from jax.experimental import pallas as pl
from jax.experimental.pallas import tpu as pltpu

F32 = jnp.float32
BF16 = jnp.bfloat16
EPS = 1e-6
NEG = -1e30

LANES = 128
SUBLANES = 8
MXU_WIDTH = 256
VMEM_LIMIT_BYTES = 48 * 1024 * 1024

RET_HEADS = 4
RET_DIM = 128
RET_CHUNK = 128
ROPE_THETA = 10000.0
LRU_C = 8.0
LRU_CONV = 4
FFN_CONV = 3
C_HEADS = 16
C_HEAD_DIM = 64
MOBA_BLOCK = 256
MOBA_TOPK = 3
REL_BUCKETS = 32
REL_MAX_DIST = 128
LOG2E = 1.4426950408889634
MASK_ROWS = 32
PV_ONES = 16
GROUP = 2
FAR_GROUP = 2
KMEAN_BLOCKS_PER_STEP = 4
PAGES_PER_BLOCK = 2


def _params(*sem):
    return pltpu.CompilerParams(dimension_semantics=sem, vmem_limit_bytes=VMEM_LIMIT_BYTES)


def _rms(x, g):
    return x * lax.rsqrt(jnp.mean(x * x, axis=-1, keepdims=True) + EPS) * g


def _gelu(x):
    return 0.5 * x * (1.0 + jnp.tanh(0.7978845608028654 * (x + 0.044715 * (x * x * x))))


def _dot(a, b):
    return jnp.dot(a, b, preferred_element_type=F32)


def _dot_nt(a, b):
    return lax.dot_general(a, b, (((1,), (1,)), ((), ())), preferred_element_type=F32)


def _dot_tn(a, b):
    return lax.dot_general(a, b, (((0,), (0,)), ((), ())), preferred_element_type=F32)


def _head_rms(y, hd):
    tm, tn = y.shape
    lo = lax.broadcasted_iota(jnp.int32, (tm, LANES), 1) < hd
    outs = []
    for s in range(tn // LANES):
        ys = y[:, s * LANES:(s + 1) * LANES]
        y2 = ys * ys
        s0 = jnp.sum(jnp.where(lo, y2, 0.0), axis=-1, keepdims=True)
        s1 = jnp.sum(jnp.where(lo, 0.0, y2), axis=-1, keepdims=True)
        r = lax.rsqrt(jnp.where(lo, s0, s1) * (1.0 / hd) + EPS)
        outs.append(ys * r)
    return jnp.concatenate(outs, axis=-1)


def _norm_proj_kernel(x_ref, g_ref, w_ref, hg_ref, *o_refs, n_rms_tiles, plan, tn):
    nout = w_ref.shape[1]
    xn = _rms(x_ref[...], g_ref[...]).astype(BF16)
    k = 0
    for t in range(nout // tn):
        cols = slice(t * tn, (t + 1) * tn)
        y = _dot(xn, w_ref[:, cols])
        if not plan:
            o_refs[0][:, cols] = y
            continue
        if t < n_rms_tiles:
            y = _head_rms(y, C_HEAD_DIM) * hg_ref[:, cols]
        for transposed in plan[t]:
            if transposed:
                o_refs[k][0] = y.T
            else:
                o_refs[k][...] = y
            k += 1


def _norm_proj(x, g, w_bf16, *, tm, tn, head_gain=None, n_rms_tiles=0, plan=None, n_seq=1):
    n, d = x.shape
    nout = w_bf16.shape[1]
    nj = nout // tn
    if head_gain is None:
        head_gain = jnp.ones((1, nout), F32)
    if plan is None:
        out_shape = [jax.ShapeDtypeStruct((n, nout), F32)]
        out_specs = [pl.BlockSpec((tm, nout), lambda i: (i, 0))]
    else:
        assert len(plan) == nj
        t_len = n // n_seq
        nt = t_len // tm
        out_shape, out_specs = [], []
        for outs in plan:
            for transposed in outs:
                if transposed:
                    out_shape.append(jax.ShapeDtypeStruct((n_seq, tn, t_len), F32))
                    out_specs.append(pl.BlockSpec((1, tn, tm), lambda i: (i // nt, 0, i % nt)))
                else:
                    out_shape.append(jax.ShapeDtypeStruct((n, tn), F32))
                    out_specs.append(pl.BlockSpec((tm, tn), lambda i: (i, 0)))
    outs = pl.pallas_call(
        functools.partial(_norm_proj_kernel, n_rms_tiles=n_rms_tiles, plan=plan, tn=tn),
        grid=(n // tm,),
        in_specs=[
            pl.BlockSpec((tm, d), lambda i: (i, 0)),
            pl.BlockSpec((1, d), lambda i: (0, 0)),
            pl.BlockSpec((d, nout), lambda i: (0, 0)),
            pl.BlockSpec((1, nout), lambda i: (0, 0)),
        ],
        out_specs=out_specs,
        out_shape=out_shape,
        compiler_params=_params("arbitrary"),
        name="norm_proj",
    )(x, g.reshape(1, d), w_bf16, head_gain)
    return outs[0] if plan is None else outs


def _ret_kernel(q_ref, k_ref, v_ref, g_ref, s0_ref, cos_ref, sin_ref, dm_ref, qd_ref, kd_ref, cd_ref,
                gg_ref, gb_ref, o_ref, s_ref, st_ref, *, c, nsub):
    t = pl.program_id(1)

    @pl.when(t == 0)
    def _():
        st_ref[...] = s0_ref[0]

    for u in range(nsub):
        rows = slice(u * c, (u + 1) * c)
        cos = cos_ref[rows, :]
        sin = sin_ref[rows, :]
        for hh in range(RET_HEADS):
            cols = slice(hh * RET_DIM, (hh + 1) * RET_DIM)
            q = q_ref[0, rows, cols]
            k = k_ref[0, rows, cols]
            q = q * cos + pltpu.roll(q, RET_DIM // 2, 1) * sin
            k = (k * cos + pltpu.roll(k, RET_DIM // 2, 1) * sin) * (RET_DIM ** -0.5)
            s = st_ref[hh]
            qb = q.astype(BF16)
            vb = v_ref[0, rows, cols].astype(BF16)
            att = _dot_nt(qb, k.astype(BF16)) * dm_ref[hh]
            o = _dot(att.astype(BF16), vb) + _dot(qb, s.astype(BF16)) * qd_ref[hh]
            st_ref[hh] = s * cd_ref[hh] + _dot_tn((k * kd_ref[hh]).astype(BF16), vb)
            mu = jnp.mean(o, axis=-1, keepdims=True)
            oc = o - mu
            var = jnp.mean(oc * oc, axis=-1, keepdims=True)
            o = oc * lax.rsqrt(var + EPS) * gg_ref[:, cols] + gb_ref[:, cols]
            gate = g_ref[0, rows, cols]
            o_ref[0, rows, cols] = o * (gate * jax.nn.sigmoid(gate))

    @pl.when(t == pl.num_programs(1) - 1)
    def _():
        s_ref[0] = st_ref[...]


def _retention(proj, s0, pos, gn_g, gn_b, *, c_eff, c_pad, rows_per_step):
    b, t, _ = proj.shape
    h = RET_HEADS
    half = RET_DIM // 2
    inv = ROPE_THETA ** (-jnp.arange(half, dtype=F32) / half)
    ang = pos.astype(F32)[:, None] * inv[None, :]
    cos2 = jnp.concatenate([jnp.cos(ang), jnp.cos(ang)], axis=-1)
    sin2 = jnp.concatenate([-jnp.sin(ang), jnp.sin(ang)], axis=-1)
    log_g = jnp.log1p(-jnp.exp2(-5.0 - jnp.arange(h, dtype=F32)))
    idx = jnp.arange(c_pad, dtype=F32)
    diff = idx[:, None] - idx[None, :]
    dmask = jnp.where(diff >= 0, jnp.exp(log_g[:, None, None] * jnp.maximum(diff, 0.0)), 0.0)
    q_dec = jnp.exp(log_g[:, None, None] * (idx[None, :, None] + 1.0))
    k_dec = jnp.exp(log_g[:, None, None] * (c_eff - 1.0 - idx[None, :, None]))
    c_dec = jnp.exp(log_g * c_eff).reshape(h, 1, 1)
    r = rows_per_step
    nsub = r // c_pad
    w = h * RET_DIM
    col = lambda part: pl.BlockSpec((1, r, w), lambda bi, ti, part=part: (bi, ti, part))
    const = lambda shape: pl.BlockSpec(shape, lambda bi, ti: (0,) * len(shape))
    o, s_new = pl.pallas_call(
        functools.partial(_ret_kernel, c=c_pad, nsub=nsub),
        grid=(b, t // r),
        in_specs=[
            col(0), col(1), col(2), col(3),
            pl.BlockSpec((1, h, RET_DIM, RET_DIM), lambda bi, ti: (bi, 0, 0, 0)),
            pl.BlockSpec((r, RET_DIM), lambda bi, ti: (ti, 0)),
            pl.BlockSpec((r, RET_DIM), lambda bi, ti: (ti, 0)),
            const((h, c_pad, c_pad)), const((h, c_pad, 1)), const((h, c_pad, 1)), const((h, 1, 1)),
            const((1, w)), const((1, w)),
        ],
        out_specs=[
            pl.BlockSpec((1, r, w), lambda bi, ti: (bi, ti, 0)),
            pl.BlockSpec((1, h, RET_DIM, RET_DIM), lambda bi, ti: (bi, 0, 0, 0)),
        ],
        out_shape=[
            jax.ShapeDtypeStruct((b, t, w), F32),
            jax.ShapeDtypeStruct((b, h, RET_DIM, RET_DIM), F32),
        ],
        scratch_shapes=[pltpu.VMEM((h, RET_DIM, RET_DIM), F32)],
        compiler_params=_params("arbitrary", "arbitrary"),
        name="retention",
    )(proj, proj, proj, proj, s0, cos2, sin2, dmask, q_dec, k_dec, c_dec,
      gn_g.reshape(1, -1), gn_b.reshape(1, -1))
    return o, s_new


def _shift_rows(x, shift, fill):
    if shift % SUBLANES == 0:
        return jnp.concatenate([jnp.full((shift, x.shape[1]), fill, x.dtype), x[:-shift]], axis=0)
    rows = lax.broadcasted_iota(jnp.int32, x.shape, 0)
    return jnp.where(rows >= shift, pltpu.roll(x, shift, 0), fill)


def _lru_kernel(bx_ref, bg_ref, tail0_ref, h0_ref, cw_ref, cb_ref, wg_ref, ba_ref, bi_ref, lam_ref,
                ob_ref, hl_ref, tail_ref, tail_sc, h_sc, *, stride, pad):
    t = pl.program_id(1)
    r, w = bx_ref.shape[1], bx_ref.shape[2]

    @pl.when(t == 0)
    def _():
        tail_sc[...] = tail0_ref[0]
        h_sc[...] = h0_ref[0]

    bx = bx_ref[0]
    xp = jnp.concatenate([tail_sc[...], bx], axis=0)
    xc = cb_ref[...]
    for j in range(LRU_CONV):
        off = pad - (LRU_CONV - 1 - j) * stride
        xc = xc + xp[off:off + r] * cw_ref[j:j + 1, :]
    new_tail = xp[r:r + pad]
    tail_sc[...] = new_tail
    tail_ref[0] = new_tail

    z = _dot(xc.astype(BF16), wg_ref[...])
    rg = jax.nn.sigmoid(z[:, :w] + ba_ref[...])
    ig = jax.nn.sigmoid(z[:, w:] + bi_ref[...])
    nl = -lam_ref[...]
    softplus = jnp.maximum(nl, 0.0) + jnp.log1p(jnp.exp(-jnp.abs(nl)))
    log_a = -LRU_C * rg * softplus
    a = jnp.exp(log_a)
    th = jnp.tanh(log_a)
    u = jnp.sqrt(-2.0 * th / (1.0 - th)) * (ig * xc)

    d = 1
    while d < r // stride:
        u = u + a * _shift_rows(u, d * stride, 0.0)
        a = a * _shift_rows(a, d * stride, 1.0)
        d *= 2
    hc = h_sc[...]
    if r > stride:
        hc = jnp.tile(hc, (r // stride, 1)) if stride > 1 else jnp.broadcast_to(hc, (r, w))
    hs = u + a * hc
    h_last = hs[r - stride:, :]
    h_sc[...] = h_last
    hl_ref[0] = h_last
    ob_ref[0] = hs * _gelu(bg_ref[0])


def _lru(proj, tail0, h0, cw, cb, wg_bf16, ba, bi, lam, *, stride, rows_per_step):
    b, t, _ = proj.shape
    w = cw.shape[1]
    pad = tail0.shape[1]
    r = rows_per_step
    vec = lambda: pl.BlockSpec((1, w), lambda bi_, ti: (0, 0))
    ob, h_last, tail = pl.pallas_call(
        functools.partial(_lru_kernel, stride=stride, pad=pad),
        grid=(b, t // r),
        in_specs=[
            pl.BlockSpec((1, r, w), lambda bi_, ti: (bi_, ti, 4)),
            pl.BlockSpec((1, r, w), lambda bi_, ti: (bi_, ti, 5)),
            pl.BlockSpec((1, pad, w), lambda bi_, ti: (bi_, 0, 0)),
            pl.BlockSpec((1, stride, w), lambda bi_, ti: (bi_, 0, 0)),
            pl.BlockSpec((LRU_CONV, w), lambda bi_, ti: (0, 0)),
            vec(),
            pl.BlockSpec((w, 2 * w), lambda bi_, ti: (0, 0)),
            vec(), vec(), vec(),
        ],
        out_specs=[
            pl.BlockSpec((1, r, w), lambda bi_, ti: (bi_, ti, 0)),
            pl.BlockSpec((1, stride, w), lambda bi_, ti: (bi_, 0, 0)),
            pl.BlockSpec((1, pad, w), lambda bi_, ti: (bi_, 0, 0)),
        ],
        out_shape=[
            jax.ShapeDtypeStruct((b, t, w), F32),
            jax.ShapeDtypeStruct((b, stride, w), F32),
            jax.ShapeDtypeStruct((b, pad, w), F32),
        ],
        scratch_shapes=[pltpu.VMEM((pad, w), F32), pltpu.VMEM((stride, w), F32)],
        compiler_params=_params("arbitrary", "arbitrary"),
        name="rglru",
    )(proj, proj, tail0, h0, cw, cb.reshape(1, w), wg_bf16, ba.reshape(1, w), bi.reshape(1, w),
      lam.reshape(1, w))
    return ob, h_last, tail


def _block_diag(wb):
    nb, c, _ = wb.shape
    eye = jnp.eye(nb, dtype=wb.dtype)
    return (eye[:, None, :, None] * wb[:, :, None, :]).reshape(nb * c, nb * c)


def _post_kernel(*refs, n_mix, stride, pad):
    h_ref = refs[0]
    a_refs = refs[1:1 + n_mix]
    wo_refs = refs[1 + n_mix:1 + 2 * n_mix]
    (p_ref, nf_ref, wg_ref, wu_ref, wout_ref, cw_ref, cb_ref, cst_ref, pgn_ref, wpg_ref, wpp_ref, ppn_ref,
     o_ref, fc_ref, h1_sc, xn_sc, acc_sc, carry_sc) = refs[1 + 2 * n_mix:]
    i = pl.program_id(1)
    j = pl.program_id(2)
    tm = h_ref.shape[0]

    @pl.when(j == 0)
    def _():
        y = h_ref[...]
        for a_ref, wo_ref in zip(a_refs, wo_refs):
            y = y + _dot(a_ref[...].astype(BF16), wo_ref[...])
        h1_sc[...] = y
        xn_sc[...] = _rms(y, nf_ref[...]).astype(BF16)
        acc_sc[...] = jnp.zeros(acc_sc.shape, F32)

    @pl.when(i == 0)
    def _():
        carry_sc[j] = cst_ref[0]

    xn = xn_sc[...]
    tf = wg_ref.shape[1]
    chunk = min(tf, MXU_WIDTH)
    contrib = None
    for c in range(tf // chunk):
        cols = slice(c * chunk, (c + 1) * chunk)
        g = _dot(xn, wg_ref[:, cols])
        u = _dot(xn, wu_ref[:, cols])
        xp = jnp.concatenate([carry_sc[j, :, cols], g], axis=0)
        gc = cb_ref[:, cols]
        for kk in range(FFN_CONV):
            off = pad - (FFN_CONV - 1 - kk) * stride
            gc = gc + xp[off:off + tm] * cw_ref[kk:kk + 1, cols]
        carry_sc[j, :, cols] = xp[tm:tm + pad]
        part = _dot((_gelu(gc) * u).astype(BF16), wout_ref[cols, :])
        contrib = part if contrib is None else contrib + part
    acc_sc[...] += contrib

    @pl.when(i == pl.num_programs(1) - 1)
    def _():
        fc_ref[0, j] = carry_sc[j]

    @pl.when(j == pl.num_programs(2) - 1)
    def _():
        h2 = h1_sc[...] + acc_sc[...]
        gate = jax.nn.sigmoid(_dot(_rms(h2, pgn_ref[...]).astype(BF16), wpg_ref[...]))
        e = _dot(p_ref[0].astype(BF16), wpp_ref[...])
        o_ref[...] = h2 + _rms(e, ppn_ref[...]) * gate


def _post_mixer(h, mix, wo_bf16, p, nf, wffn_in_bf16, wffn_out_bf16, cw, cb, cst, pgn, wpg_bf16, wpp_bf16,
                ppn, *, p_layer, n_seq, stride, tm, tf):
    n, d = h.shape
    dff = wffn_out_bf16.shape[0]
    nj = dff // tf
    pad = cst.shape[1]
    ni = n // n_seq // tm
    n_mix = len(mix)
    row = lambda width: pl.BlockSpec((tm, width), lambda b, i, j: (b * ni + i, 0))
    full = lambda shape: pl.BlockSpec(shape, lambda b, i, j: (0, 0))
    in_specs = (
        [row(d)] + [row(a.shape[1]) for a in mix] + [full(w_.shape) for w_ in wo_bf16]
        + [
            pl.BlockSpec((1, tm, p.shape[2]), lambda b, i, j: (p_layer, b * ni + i, 0)),
            full((1, d)),
            pl.BlockSpec((d, tf), lambda b, i, j: (0, j)),
            pl.BlockSpec((d, tf), lambda b, i, j: (0, nj + j)),
            pl.BlockSpec((tf, d), lambda b, i, j: (j, 0)),
            pl.BlockSpec((FFN_CONV, tf), lambda b, i, j: (0, j)),
            pl.BlockSpec((1, tf), lambda b, i, j: (0, j)),
            pl.BlockSpec((1, pad, tf), lambda b, i, j: (b, 0, j)),
            full((1, d)),
            full(wpg_bf16.shape),
            full(wpp_bf16.shape),
            full((1, d)),
        ])
    out, fc = pl.pallas_call(
        functools.partial(_post_kernel, n_mix=n_mix, stride=stride, pad=pad),
        grid=(n_seq, ni, nj),
        in_specs=in_specs,
        out_specs=[
            pl.BlockSpec((tm, d), lambda b, i, j: (b * ni + i, 0)),
            pl.BlockSpec((1, nj, pad, tf), lambda b, i, j: (b, 0, 0, 0)),
        ],
        out_shape=[
            jax.ShapeDtypeStruct((n, d), F32),
            jax.ShapeDtypeStruct((n_seq, nj, pad, tf), F32),
        ],
        scratch_shapes=[
            pltpu.VMEM((tm, d), F32),
            pltpu.VMEM((tm, d), BF16),
            pltpu.VMEM((tm, d), F32),
            pltpu.VMEM((nj, pad, tf), F32),
        ],
        compiler_params=_params("arbitrary", "arbitrary", "arbitrary"),
        name="post_mixer",
    )(h, *mix, *wo_bf16, p, nf.reshape(1, d), wffn_in_bf16, wffn_in_bf16, wffn_out_bf16, cw,
      cb.reshape(1, dff), cst, pgn.reshape(1, d), wpg_bf16, wpp_bf16, ppn.reshape(1, d))
    return out, jnp.transpose(fc, (0, 2, 1, 3)).reshape(n_seq, pad, dff)


def _t5_bucket(dist):
    n = jnp.maximum(dist, 0)
    max_exact = REL_BUCKETS // 2
    nf = jnp.maximum(n, max_exact).astype(F32)
    large = max_exact + (jnp.log(nf / max_exact) / math.log(REL_MAX_DIST / max_exact)
                         * (REL_BUCKETS - max_exact)).astype(jnp.int32)
    large = jnp.minimum(large, REL_BUCKETS - 1)
    return jnp.where(n < max_exact, n, large)


def _bias_table(rel_bias, dist):
    onehot = (_t5_bucket(jnp.asarray(dist))[..., None] == jnp.arange(REL_BUCKETS)).astype(F32)
    return jnp.einsum('...b,bh->h...', onehot, rel_bias.astype(F32), precision=lax.Precision.HIGHEST)


def _top3(sc, idxf, n, axis=-1):
    picks = []
    for _ in range(MOBA_TOPK):
        m = jnp.max(sc, axis=axis, keepdims=True)
        first = jnp.min(jnp.where(sc == m, idxf, float(n)), axis=axis, keepdims=True)
        picks.append((first, m))
        sc = jnp.where(idxf == first, -jnp.inf, sc)
    return picks


def _moba_prompt_kernel(pt_ref, q_ref, qn_ref, k_ref, vt_ref, bo_ref, bp_ref, bf_ref, *rest, pages_per_step):
    if pages_per_step:
        ck_ref, o_ref, kmo_ref, km_sc, mask_sc, s0_sc, s1_sc, pbuf, psem = rest
    else:
        o_ref, km_sc, mask_sc, s0_sc, s1_sc = rest
    i = pl.program_id(2)
    nb = km_sc.shape[0]
    blk = MOBA_BLOCK
    hd = C_HEAD_DIM
    ext = hd - MASK_ROWS

    if pages_per_step:
        step = (pl.program_id(0) * pl.num_programs(1) + pl.program_id(1)) * pl.num_programs(2) + i
        steps_per_seq = pt_ref.shape[1] // pages_per_step
        seq = step // steps_per_seq
        chunk = step % steps_per_seq

        def page_copy(kk):
            return pltpu.make_async_copy(ck_ref.at[pt_ref[seq, chunk * pages_per_step + kk]], pbuf.at[kk],
                                         psem.at[kk])

        for kk in range(pages_per_step):
            page_copy(kk).start()

    @pl.when(i == 0)
    def _():
        def body(jb, c):
            kb = k_ref[0, pl.ds(pl.multiple_of(jb * blk, blk), blk), :]
            km_sc[pl.ds(jb, 1), :] = jnp.sum(kb, axis=0, keepdims=True) * (1.0 / blk)
            return c
        lax.fori_loop(0, nb, body, 0)
        r = lax.broadcasted_iota(jnp.int32, (2, MASK_ROWS, blk), 1)
        mask_sc[...] = jnp.where(r < nb, NEG, 0.0)

    def split_bf16(x):
        hi = x.astype(BF16)
        return hi, (x - hi.astype(F32)).astype(BF16)

    def mask_rows(kmh, q_hi, q_lo, tile):
        rowf = lax.broadcasted_iota(jnp.int32, (nb, blk), 0).astype(F32)
        k_hi, k_lo = split_bf16(kmh)
        sc = _dot(jnp.concatenate([k_hi, k_hi, k_lo], axis=1), jnp.concatenate([q_hi, q_lo, q_hi], axis=0))
        sc = jnp.where(rowf < tile.astype(F32), sc, -jnp.inf)
        sel = jnp.zeros((nb, blk), F32)
        for first, m in _top3(sc, rowf, nb, axis=0):
            sel = jnp.where((rowf == first) & (m > -jnp.inf), 1.0, sel)
        mask = jnp.where(sel > 0.0, 0.0, NEG)
        if nb < MASK_ROWS:
            mask = jnp.concatenate([mask, jnp.zeros((MASK_ROWS - nb, blk), F32)], axis=0)
        return mask

    def bias_rows(bias):
        hi = bias.astype(BF16).astype(F32)
        r = lax.broadcasted_iota(jnp.int32, (ext, blk), 0)
        return jnp.where(r == 0, hi, jnp.where(r == 1, bias - hi, 0.0))

    q_t = q_ref[0].T
    c = hd ** -0.5 * LOG2E
    qp_a = jnp.concatenate([q_t[:hd] * c, mask_sc[0], bias_rows(bf_ref[0])], axis=0).astype(BF16)
    qp_b = jnp.concatenate([mask_sc[1], bias_rows(bf_ref[1]), q_t[hd:] * c], axis=0).astype(BF16)

    def logits(jb, n_blk, block_ids, far):
        start = pl.multiple_of(jb * blk, blk)
        lane = lax.broadcasted_iota(jnp.int32, (1, LANES), 1)
        lo = lax.broadcasted_iota(jnp.int32, (blk, LANES), 1) < hd
        kbs_a, kbs_b = [], []
        for g in range(n_blk):
            kb = k_ref[0, pl.ds(pl.multiple_of(start + g * blk, blk), blk), :]
            e_a = jnp.zeros((1, LANES), F32)
            e_b = jnp.zeros((1, LANES), F32)
            if block_ids is not None:
                e_a = jnp.where(lane == hd + block_ids[g], 1.0, e_a)
                e_b = jnp.where(lane == block_ids[g], 1.0, e_b)
            if far:
                e_a = jnp.where((lane == hd + MASK_ROWS) | (lane == hd + MASK_ROWS + 1), 1.0, e_a)
                e_b = jnp.where((lane == MASK_ROWS) | (lane == MASK_ROWS + 1), 1.0, e_b)
            kbs_a.append(jnp.where(lo, kb, e_a).astype(BF16))
            kbs_b.append(jnp.where(lo, e_b, kb).astype(BF16))
        kb_a = jnp.concatenate(kbs_a, axis=0) if n_blk > 1 else kbs_a[0]
        kb_b = jnp.concatenate(kbs_b, axis=0) if n_blk > 1 else kbs_b[0]
        return _dot(kb_a, qp_a), _dot(kb_b, qp_b)

    def fold(carry, s_a, s_b, jb, n_blk):
        m_a, acc_a, m_b, acc_b = carry
        keys = n_blk * blk
        vt = vt_ref[0, :, pl.ds(pl.multiple_of(jb * blk, blk), keys)]
        ones = jnp.ones((PV_ONES, keys), F32)
        vt_a = jnp.concatenate([vt[:hd], ones], axis=0).astype(BF16)
        vt_b = jnp.concatenate([vt[hd:], ones], axis=0).astype(BF16)
        mn_a = jnp.maximum(m_a, jnp.max(s_a, axis=0, keepdims=True))
        mn_b = jnp.maximum(m_b, jnp.max(s_b, axis=0, keepdims=True))
        p_a = jnp.exp2(s_a - mn_a).astype(BF16)
        p_b = jnp.exp2(s_b - mn_b).astype(BF16)
        acc_a = jnp.exp2(m_a - mn_a) * acc_a + _dot(vt_a, p_a)
        acc_b = jnp.exp2(m_b - mn_b) * acc_b + _dot(vt_b, p_b)
        return mn_a, acc_a, mn_b, acc_b

    qn_t = qn_ref[0].T
    km = km_sc[...]
    km_lo = lax.broadcasted_iota(jnp.int32, (nb, LANES), 1) < hd
    qn_hi, qn_lo = split_bf16(qn_t)
    mask_next_a = mask_rows(jnp.where(km_lo, km, 0.0), qn_hi, qn_lo, i + 1)
    mask_next_b = mask_rows(jnp.where(km_lo, 0.0, km), qn_hi, qn_lo, i + 1)

    m0 = jnp.full((1, blk), NEG, F32)
    acc0 = jnp.zeros((hd + PV_ONES, blk), F32)
    carry = (m0, acc0, m0, acc0)
    no_mask = -LANES
    first = i == 0
    near = jnp.maximum(i - 1, 0)
    s_a, s_b = logits(near, GROUP, [jnp.where(first, no_mask, i - 1), jnp.where(first, nb - 1, no_mask)], False)
    s0_sc[0] = s_a + jnp.concatenate([jnp.where(first, bo_ref[0], bp_ref[0]), bo_ref[0]], axis=0)
    s0_sc[1] = s_b + jnp.concatenate([jnp.where(first, bo_ref[1], bp_ref[1]), bo_ref[1]], axis=0)

    n_far = jnp.maximum(i - 1, 0)
    n_grp = (n_far + FAR_GROUP - 1) // FAR_GROUP
    far_keys = FAR_GROUP * blk

    def far_logits(grp, s_sc):
        jb = jnp.minimum(grp * FAR_GROUP, nb - FAR_GROUP)
        ids = [jnp.where(grp * FAR_GROUP + g < n_far, grp * FAR_GROUP + g, nb - 1) for g in range(FAR_GROUP)]
        s_sc[0, :far_keys], s_sc[1, :far_keys] = logits(jb, FAR_GROUP, ids, True)

    def far_fold(cy, grp, s_sc):
        return fold(cy, s_sc[0, :far_keys], s_sc[1, :far_keys], jnp.minimum(grp * FAR_GROUP, nb - FAR_GROUP),
                    FAR_GROUP)

    far_logits(0, s1_sc)
    carry = fold(carry, s0_sc[0], s0_sc[1], near, GROUP)
    mask_sc[0] = mask_next_a
    mask_sc[1] = mask_next_b

    def far_pair(it, cy):
        far_logits(2 * it + 1, s0_sc)
        cy = far_fold(cy, 2 * it, s1_sc)
        far_logits(2 * it + 2, s1_sc)
        return far_fold(cy, 2 * it + 1, s0_sc)

    m_a, acc_a, m_b, acc_b = lax.fori_loop(0, (n_grp + 1) // 2, far_pair, carry)
    out_t = jnp.concatenate([acc_a[:hd] / acc_a[hd:hd + 1], acc_b[:hd] / acc_b[hd:hd + 1]], axis=0)
    o_ref[0] = out_t.T

    if pages_per_step:
        @pl.when(chunk == 0)
        def _():
            kmo_ref[...] = jnp.zeros(kmo_ref.shape, F32)

        for kk in range(pages_per_step):
            page_copy(kk).wait()
        means = kmo_ref[0]
        lane = lax.broadcasted_iota(jnp.int32, means.shape, 2)
        blocks_per_step = pages_per_step // PAGES_PER_BLOCK
        for g in range(blocks_per_step):
            tot = pbuf[g * PAGES_PER_BLOCK]
            for kk in range(1, PAGES_PER_BLOCK):
                tot = tot + pbuf[g * PAGES_PER_BLOCK + kk]
            col = jnp.sum(tot, axis=-1, keepdims=True) * (1.0 / MOBA_BLOCK)
            means = jnp.where(lane == chunk * blocks_per_step + g, col, means)
        kmo_ref[0] = means


def _moba_prompt(q, k, v_t, rel_bias, paged=None):
    b, t, width = q.shape
    blk = MOBA_BLOCK
    nb = t // blk
    assert nb <= MASK_ROWS
    hp = width // LANES
    pages_per_step = 0
    page_table = jnp.zeros((1, 1), jnp.int32)
    if paged is not None:
        ck_t, pt = paged
        n_seq2, n_pages = pt.shape
        pps, rem = divmod(n_seq2 * n_pages, b * hp * nb)
        if rem == 0 and pps > 0 and pps % PAGES_PER_BLOCK == 0 and n_pages % pps == 0:
            pages_per_step, page_table = pps, pt
    ll = np.arange(blk)[:, None]
    ii = np.arange(blk)[None, :]
    bias_own = jnp.where(ii >= ll, _bias_table(rel_bias, np.maximum(ii - ll, 0)) * LOG2E, NEG)
    bias_prev = _bias_table(rel_bias, blk + ii - ll) * LOG2E
    bias_far = (_bias_table(rel_bias, np.array([2 * blk])) * LOG2E).reshape(-1, 1, 1)
    pair = lambda shape: pl.BlockSpec((2,) + shape, lambda bi, hi, qi, pt: (hi, 0, 0))
    in_specs = [
        pl.BlockSpec((1, blk, LANES), lambda bi, hi, qi, pt: (bi, qi, hi)),
        pl.BlockSpec((1, blk, LANES), lambda bi, hi, qi, pt: (bi, jnp.minimum(qi + 1, nb - 1), hi)),
        pl.BlockSpec((1, t, LANES), lambda bi, hi, qi, pt: (bi, 0, hi)),
        pl.BlockSpec((1, LANES, t), lambda bi, hi, qi, pt: (bi, hi, 0)),
        pair((blk, blk)), pair((blk, blk)), pair((1, 1)),
    ]
    out_specs = [pl.BlockSpec((1, blk, LANES), lambda bi, hi, qi, pt: (bi, qi, hi))]
    out_shape = [jax.ShapeDtypeStruct((b, t, width), F32)]
    scratch_shapes = [pltpu.VMEM((nb, LANES), F32),
                      pltpu.VMEM((2, MASK_ROWS, blk), F32),
                      pltpu.VMEM((2, GROUP * blk, blk), F32),
                      pltpu.VMEM((2, GROUP * blk, blk), F32)]
    args = [page_table, q, q, k, v_t, bias_own, bias_prev, bias_far]
    if pages_per_step:
        _, h2, dh2, page = ck_t.shape
        steps_per_seq = n_pages // pages_per_step
        in_specs.append(pl.BlockSpec(memory_space=pl.ANY))
        out_specs.append(pl.BlockSpec((1, h2, dh2, n_pages // PAGES_PER_BLOCK),
                                      lambda bi, hi, qi, pt: (((bi * hp + hi) * nb + qi) // steps_per_seq, 0, 0, 0)))
        out_shape.append(jax.ShapeDtypeStruct((n_seq2, h2, dh2, n_pages // PAGES_PER_BLOCK), F32))
        scratch_shapes += [pltpu.VMEM((pages_per_step, h2, dh2, page), F32),
                           pltpu.SemaphoreType.DMA((pages_per_step,))]
        args.append(ck_t)
    outs = pl.pallas_call(
        functools.partial(_moba_prompt_kernel, pages_per_step=pages_per_step),
        grid_spec=pltpu.PrefetchScalarGridSpec(
            num_scalar_prefetch=1, grid=(b, hp, nb), in_specs=in_specs, out_specs=out_specs,
            scratch_shapes=scratch_shapes),
        out_shape=out_shape,
        compiler_params=_params("arbitrary", "arbitrary", "arbitrary"),
        name="moba_prompt",
    )(*args)
    return (outs[0], outs[1]) if pages_per_step else (outs[0], None)


def _kmean_kernel(pt_ref, *refs):
    page_refs, o_ref = refs[:-1], refs[-1]
    j = pl.program_id(1)

    @pl.when(j == 0)
    def _():
        o_ref[...] = jnp.zeros(o_ref.shape, F32)

    lane = lax.broadcasted_iota(jnp.int32, o_ref.shape[1:], 2)
    out = o_ref[0]
    for g in range(len(page_refs) // PAGES_PER_BLOCK):
        pages = page_refs[g * PAGES_PER_BLOCK:(g + 1) * PAGES_PER_BLOCK]
        tot = pages[0][0]
        for pr in pages[1:]:
            tot = tot + pr[0]
        col = jnp.sum(tot, axis=-1, keepdims=True) * (1.0 / MOBA_BLOCK)
        out = jnp.where(lane == j * KMEAN_BLOCKS_PER_STEP + g, col, out)
    o_ref[0] = out


def _block_means(ck_t, page_table, nb):
    _, h, dh, page = ck_t.shape
    b = page_table.shape[0]
    g = KMEAN_BLOCKS_PER_STEP
    assert nb % g == 0
    n_pages = g * PAGES_PER_BLOCK
    page_spec = lambda k: pl.BlockSpec((1, h, dh, page), lambda bi, j, pt: (pt[bi, n_pages * j + k], 0, 0, 0))
    return pl.pallas_call(
        _kmean_kernel,
        grid_spec=pltpu.PrefetchScalarGridSpec(
            num_scalar_prefetch=1,
            grid=(b, nb // g),
            in_specs=[page_spec(k) for k in range(n_pages)],
            out_specs=pl.BlockSpec((1, h, dh, nb), lambda bi, j, pt: (bi, 0, 0, 0)),
        ),
        out_shape=jax.ShapeDtypeStruct((b, h, dh, nb), F32),
        compiler_params=_params("arbitrary", "arbitrary"),
        name="moba_block_means",
    )(page_table, *([ck_t] * n_pages))


def _select_kernel(qt_ref, km_ref, o_ref):
    s_len, _, h = qt_ref.shape[1:]
    nb = km_ref.shape[3]
    rows = s_len * h
    rid = lax.broadcasted_iota(jnp.int32, (rows, nb), 0)
    sc = jnp.zeros((rows, nb), F32)
    for s in range(s_len):
        qt = qt_ref[0, s]
        for hh in range(h):
            sc = jnp.where(rid == s * h + hh, jnp.sum(km_ref[0, hh] * qt[:, hh:hh + 1], axis=0, keepdims=True), sc)
    colf = lax.broadcasted_iota(jnp.int32, (rows, nb), 1).astype(F32)
    picks = [jnp.minimum(first, nb - 1.0).astype(jnp.int32) for first, _ in _top3(sc, colf, nb)]
    lane = lax.broadcasted_iota(jnp.int32, (rows, LANES), 1)
    o_ref[0] = jnp.where(lane == 0, picks[0], jnp.where(lane == 1, picks[1], picks[2]))


def _select_blocks(q_t, kmean_t):
    b, s_len, dh, h = q_t.shape
    nb = kmean_t.shape[3]
    out = pl.pallas_call(
        _select_kernel,
        grid=(b,),
        in_specs=[
            pl.BlockSpec((1, s_len, dh, h), lambda bi: (bi, 0, 0, 0)),
            pl.BlockSpec((1, h, dh, nb), lambda bi: (bi, 0, 0, 0)),
        ],
        out_specs=pl.BlockSpec((1, s_len * h, LANES), lambda bi: (bi, 0, 0)),
        out_shape=jax.ShapeDtypeStruct((b, s_len * h, LANES), jnp.int32),
        compiler_params=_params("arbitrary"),
        name="moba_select",
    )(q_t, kmean_t)
    return out[:, :, :MOBA_TOPK].reshape(b, s_len, h, MOBA_TOPK)


def _moba_sample_kernel(pt_ref, sel_ref, ck_ref, cv_ref, qt_ref, kn_ref, vn_ref, bl_ref, bo_ref, bf_ref,
                        o_ref, kbuf, vbuf, sem, *, nb):
    b = pl.program_id(0)
    s = pl.program_id(1)
    s_len = pl.num_programs(1)
    _, h, _, dh, _ = kbuf.shape
    ppb = PAGES_PER_BLOCK
    step = b * s_len + s
    cur = step % 2

    def copies(st, buf):
        out = []
        for hh in range(h):
            for n in range(MOBA_TOPK):
                idx = None if st is None else sel_ref[(st * h + hh) * MOBA_TOPK + n]
                for pg in range(ppb):
                    phys = 0 if st is None else pt_ref[st // s_len, idx * ppb + pg]
                    slot = n * ppb + pg
                    out.append(pltpu.make_async_copy(ck_ref.at[phys, hh], kbuf.at[buf, hh, slot], sem.at[buf, 0]))
                    out.append(pltpu.make_async_copy(cv_ref.at[phys, hh], vbuf.at[buf, hh, slot], sem.at[buf, 1]))
        return out

    @pl.when(step == 0)
    def _():
        for cp in copies(step, cur):
            cp.start()

    @pl.when(step + 1 < pl.num_programs(0) * s_len)
    def _():
        for cp in copies(step + 1, 1 - cur):
            cp.start()

    for cp in copies(None, cur):
        cp.wait()

    kbuf = kbuf.at[cur]
    vbuf = vbuf.at[cur]
    lane = lax.broadcasted_iota(jnp.int32, (dh, LANES), 1)
    qt = qt_ref[0, 0] * (dh ** -0.5)
    out = jnp.zeros((dh, LANES), F32)
    for hh in range(h):
        qc = qt[:, hh:hh + 1]
        far = bf_ref[hh:hh + 1, :]
        logits = []
        for n in range(MOBA_TOPK):
            idx = sel_ref[((b * s_len + s) * h + hh) * MOBA_TOPK + n]
            for pg in range(ppb):
                lg = jnp.sum(kbuf[hh, n * ppb + pg] * qc, axis=0, keepdims=True)
                logits.append(lg + jnp.where(idx == nb - 1, bl_ref[0, hh, pg:pg + 1, :], far))
        lown = jnp.sum(kn_ref[0, hh] * qc, axis=0, keepdims=True) + bo_ref[0, hh:hh + 1, :]
        m = jnp.max(lown, axis=-1, keepdims=True)
        for lg in logits:
            m = jnp.maximum(m, jnp.max(lg, axis=-1, keepdims=True))
        pown = jnp.exp(lown - m)
        den = jnp.sum(pown, axis=-1, keepdims=True)
        col = jnp.sum(vn_ref[0, hh] * pown, axis=-1, keepdims=True)
        psum = None
        pv = None
        for slot, lg in enumerate(logits):
            p = jnp.exp(lg - m)
            psum = p if psum is None else psum + p
            pv = vbuf[hh, slot] * p if pv is None else pv + vbuf[hh, slot] * p
        den = den + jnp.sum(psum, axis=-1, keepdims=True)
        col = col + jnp.sum(pv, axis=-1, keepdims=True)
        out = jnp.where(lane == hh, col / den, out)
    o_ref[0, 0] = out


def _moba_sample(q, k_new, v_new, ck_t, cv_t, page_table, rel_bias, past_len, kmean_t=None):
    b, s_len, h, dh = q.shape
    page = ck_t.shape[3]
    blk = MOBA_BLOCK
    assert past_len % blk == 0 and blk == PAGES_PER_BLOCK * page
    nb = past_len // blk
    assert nb >= MOBA_TOPK
    q_t = jnp.transpose(q, (0, 1, 3, 2))
    if kmean_t is None:
        kmean_t = _block_means(ck_t, page_table, nb)
    sel = _select_blocks(q_t, kmean_t)

    ss = np.arange(s_len)
    bias_last = _bias_table(rel_bias, blk + ss[:, None] - np.arange(blk)[None, :])
    bias_last = jnp.transpose(bias_last, (1, 0, 2)).reshape(s_len, h, PAGES_PER_BLOCK, page)
    own_d = ss[:, None] - ss[None, :]
    bias_own = jnp.where(own_d >= 0, _bias_table(rel_bias, np.maximum(own_d, 0)), NEG)
    bias_own = jnp.transpose(bias_own, (1, 0, 2))
    bias_far = _bias_table(rel_bias, np.array([2 * blk]))
    kn_t = jnp.transpose(k_new, (0, 2, 3, 1))
    vn_t = jnp.transpose(v_new, (0, 2, 3, 1))

    out = pl.pallas_call(
        functools.partial(_moba_sample_kernel, nb=nb),
        grid_spec=pltpu.PrefetchScalarGridSpec(
            num_scalar_prefetch=2,
            grid=(b, s_len),
            in_specs=[
                pl.BlockSpec(memory_space=pl.ANY),
                pl.BlockSpec(memory_space=pl.ANY),
                pl.BlockSpec((1, 1, dh, h), lambda bi, si, pt, sl: (bi, si, 0, 0)),
                pl.BlockSpec((1, h, dh, s_len), lambda bi, si, pt, sl: (bi, 0, 0, 0)),
                pl.BlockSpec((1, h, dh, s_len), lambda bi, si, pt, sl: (bi, 0, 0, 0)),
                pl.BlockSpec((1, h, PAGES_PER_BLOCK, page), lambda bi, si, pt, sl: (si, 0, 0, 0)),
                pl.BlockSpec((1, h, s_len), lambda bi, si, pt, sl: (si, 0, 0)),
                pl.BlockSpec((h, 1), lambda bi, si, pt, sl: (0, 0)),
            ],
            out_specs=pl.BlockSpec((1, 1, dh, LANES), lambda bi, si, pt, sl: (bi, si, 0, 0)),
            scratch_shapes=[
                pltpu.VMEM((2, h, PAGES_PER_BLOCK * MOBA_TOPK, dh, page), F32),
                pltpu.VMEM((2, h, PAGES_PER_BLOCK * MOBA_TOPK, dh, page), F32),
                pltpu.SemaphoreType.DMA((2, 2)),
            ],
        ),
        out_shape=jax.ShapeDtypeStruct((b, s_len, dh, LANES), F32),
        compiler_params=_params("arbitrary", "arbitrary"),
        name="moba_sample",
    )(page_table, sel.reshape(-1), ck_t, cv_t, q_t, kn_t, vn_t, bias_last, bias_own, bias_far)
    return jnp.transpose(out[:, :, :, :h], (0, 1, 3, 2))


def _prep_weights(w):
    bf = lambda a: a.astype(BF16)
    n_even = w['w0_in'].shape[0]
    n_odd = w['c_in'].shape[0]
    out = dict(w)
    out['w0_in'] = bf(w['w0_in'])
    out['w0_out'] = bf(w['w0_out'])
    out['lru_wg'] = [bf(jnp.concatenate([_block_diag(w['lru_wa'][e]), _block_diag(w['lru_wi'][e])], axis=1))
                     for e in range(n_even)]
    out['c_in'] = bf(w['c_in'])
    out['c_out'] = bf(w['c_out'])
    out['c_gain'] = [jnp.concatenate([jnp.tile(w['c_qn'][o], C_HEADS), jnp.tile(w['c_kn'][o], C_HEADS),
                                      jnp.ones((C_HEADS * C_HEAD_DIM,), F32)]).reshape(1, -1)
                     for o in range(n_odd)]
    out['ffn_in'] = bf(w['ffn_in'])
    out['ffn_out'] = bf(w['ffn_out'])
    out['ple_proj'] = bf(w['ple_proj'])
    out['ple_gate'] = bf(w['ple_gate'])
    return out


def _post(w, l, h, mix, wo, p, p_layer, cst, *, n_seq, stride, tm, tf):
    return _post_mixer(h, mix, wo, p, w['norm_ffn'][l], w['ffn_in'][l], w['ffn_out'][l], w['ffn_conv_w'][l],
                       w['ffn_conv_b'][l], cst, w['ple_gate_norm'][l], w['ple_gate'][l], w['ple_proj'][l],
                       w['ple_post_norm'][l], p_layer=p_layer, n_seq=n_seq, stride=stride, tm=tm, tf=tf)


def _prompt_trunk(x, p, w, rel_bias, ck_ts, page_table):
    b, t, d = x.shape
    n = b * t
    depth = p.shape[0]
    h = x.reshape(n, d)
    tm = 512
    rets, lrus, lconvs, fconvs, ks, vs, kmeans = [], [], [], [], [], [], []
    lw = w['lru_conv_w'].shape[2]
    dff = w['ffn_out'].shape[1]
    for l in range(depth):
        cst = jnp.zeros((b, SUBLANES, dff), F32)
        if l % 2 == 0:
            e = l // 2
            proj = _norm_proj(h, w['norm_mix'][l], w['w0_in'][e], tm=512, tn=1024).reshape(b, t, -1)
            o, s_new = _retention(proj, jnp.zeros((b, RET_HEADS, RET_DIM, RET_DIM), F32), jnp.arange(t),
                                  w['ret_gn_g'][e], w['ret_gn_b'][e], c_eff=RET_CHUNK, c_pad=RET_CHUNK,
                                  rows_per_step=512)
            ob, h_last, tail = _lru(proj, jnp.zeros((b, SUBLANES, lw), F32), jnp.zeros((b, 1, lw), F32),
                                    w['lru_conv_w'][e], w['lru_conv_b'][e], w['lru_wg'][e], w['lru_ba'][e],
                                    w['lru_bi'][e], w['lru_lam'][e], stride=1, rows_per_step=512)
            rets.append(s_new)
            lrus.append(h_last[:, 0])
            lconvs.append(tail[:, SUBLANES - (LRU_CONV - 1):])
            rw = o.shape[-1]
            mix = [o.reshape(n, rw), ob.reshape(n, lw)]
            wo = [w['w0_out'][e][:rw], w['w0_out'][e][rw:]]
        else:
            o_ = l // 2
            q, k, k_t, v_t = _norm_proj(h, w['norm_mix'][l], w['c_in'][o_], tm=512, tn=1024,
                                        head_gain=w['c_gain'][o_], n_rms_tiles=2,
                                        plan=((False,), (False, True), (True,)), n_seq=b)
            cw = q.shape[1]
            att, kmean_t = _moba_prompt(q.reshape(b, t, cw), k.reshape(b, t, cw), v_t, rel_bias,
                                        paged=(ck_ts[o_], page_table))
            kmeans.append(kmean_t)
            heads = lambda a: jnp.transpose(a.reshape(b, C_HEADS, C_HEAD_DIM, t), (0, 3, 1, 2))
            ks.append(heads(k_t))
            vs.append(heads(v_t))
            mix = [att.reshape(n, cw)]
            wo = [w['c_out'][o_]]
        h, fc = _post(w, l, h, mix, wo, p.reshape(depth, n, -1), l, cst, n_seq=b, stride=1, tm=tm, tf=512)
        fconvs.append(fc[:, SUBLANES - (FFN_CONV - 1):])
    return (h.reshape(b, t, d), jnp.stack(rets), jnp.stack(lrus), jnp.stack(lconvs), jnp.stack(fconvs),
            jnp.stack(ks), jnp.stack(vs), kmeans)


def _sample_trunk(x, p, ret0, lru0, lconv0, fconv0, ck_ts, cv_ts, page_table, w, rel_bias, kmeans):
    b, t, d = x.shape
    n = b * t
    depth = p.shape[0]
    past_len = page_table.shape[1] * ck_ts[0].shape[3]
    tmaj = lambda a: jnp.transpose(a, (1, 0) + tuple(range(2, a.ndim)))
    h = tmaj(x).reshape(n, d)
    rets, lrus, lconvs, fconvs, ks, vs = [], [], [], [], [], []
    lw = w['lru_conv_w'].shape[2]
    for l in range(depth):
        cst = tmaj(fconv0[l]).reshape(1, (FFN_CONV - 1) * b, -1)
        if l % 2 == 0:
            e = l // 2
            proj = _norm_proj(h, w['norm_mix'][l], w['w0_in'][e], tm=n, tn=1024)
            rcols = 4 * RET_HEADS * RET_DIM
            pr = tmaj(proj[:, :rcols].reshape(t, b, rcols))
            pr = jnp.pad(pr, ((0, 0), (0, SUBLANES - t), (0, 0)))
            pos = past_len + jnp.arange(SUBLANES)
            o, s_new = _retention(pr, ret0[e].astype(F32), pos, w['ret_gn_g'][e], w['ret_gn_b'][e],
                                  c_eff=t, c_pad=SUBLANES, rows_per_step=SUBLANES)
            o = tmaj(o[:, :t]).reshape(n, -1)
            tail0 = tmaj(lconv0[e]).reshape(1, (LRU_CONV - 1) * b, lw)
            ob, h_last, tail = _lru(proj.reshape(1, n, -1), tail0, lru0[e].reshape(1, b, lw),
                                    w['lru_conv_w'][e], w['lru_conv_b'][e], w['lru_wg'][e], w['lru_ba'][e],
                                    w['lru_bi'][e], w['lru_lam'][e], stride=b, rows_per_step=n)
            rets.append(s_new)
            lrus.append(h_last[0])
            lconvs.append(tmaj(tail.reshape(LRU_CONV - 1, b, lw)))
            rw = o.shape[-1]
            mix = [o, ob.reshape(n, lw)]
            wo = [w['w0_out'][e][:rw], w['w0_out'][e][rw:]]
        else:
            o_ = l // 2
            q, k, v = _norm_proj(h, w['norm_mix'][l], w['c_in'][o_], tm=n, tn=1024, head_gain=w['c_gain'][o_],
                                 n_rms_tiles=2, plan=((False,), (False,), (False,)))
            heads = lambda a: tmaj(a.reshape(t, b, C_HEADS, C_HEAD_DIM))
            q, k, v = heads(q), heads(k), heads(v)
            att = _moba_sample(q, k, v, ck_ts[o_], cv_ts[o_], page_table, rel_bias, past_len, kmean_t=kmeans[o_])
            ks.append(k)
            vs.append(v)
            mix = [tmaj(att).reshape(n, -1)]
            wo = [w['c_out'][o_]]
        h, fc = _post(w, l, h, mix, wo, tmaj(p[l]).reshape(1, n, -1), 0, cst, n_seq=1, stride=b, tm=n, tf=512)
        fconvs.append(tmaj(fc.reshape(FFN_CONV - 1, b, -1)))
    return (tmaj(h.reshape(t, b, d)), jnp.stack(rets), jnp.stack(lrus), jnp.stack(lconvs), jnp.stack(fconvs),
            jnp.stack(ks), jnp.stack(vs))


def kernel(x_prompt, x_sample, state_ret, state_lru, state_lru_conv, state_ffn_conv, cache_k, cache_v, page_table, p_prompt, p_sample, norm_mix, norm_ffn, w0_in, ret_gn_g, ret_gn_b, lru_conv_w, lru_conv_b, lru_wa, lru_ba, lru_wi, lru_bi, lru_lam, w0_out, c_in, c_qn, c_kn, c_out, rel_bias, ffn_in, ffn_conv_w, ffn_conv_b, ffn_out, ple_proj, ple_gate_norm, ple_gate, ple_post_norm):
    w = _prep_weights(dict(
        norm_mix=norm_mix, norm_ffn=norm_ffn, w0_in=w0_in, ret_gn_g=ret_gn_g, ret_gn_b=ret_gn_b,
        lru_conv_w=lru_conv_w, lru_conv_b=lru_conv_b, lru_wa=lru_wa, lru_ba=lru_ba, lru_wi=lru_wi,
        lru_bi=lru_bi, lru_lam=lru_lam, w0_out=w0_out, c_in=c_in, c_qn=c_qn, c_kn=c_kn, c_out=c_out,
        ffn_in=ffn_in, ffn_conv_w=ffn_conv_w, ffn_conv_b=ffn_conv_b, ffn_out=ffn_out, ple_proj=ple_proj,
        ple_gate_norm=ple_gate_norm, ple_gate=ple_gate, ple_post_norm=ple_post_norm))
    ck_ts = [jnp.transpose(cache_k[o], (0, 2, 3, 1)) for o in range(cache_k.shape[0])]
    cv_ts = [jnp.transpose(cache_v[o], (0, 2, 3, 1)) for o in range(cache_v.shape[0])]
    y_p, ret_p, lru_p, lconv_p, fconv_p, k_p, v_p, kmeans = _prompt_trunk(
        x_prompt, p_prompt, w, rel_bias, ck_ts, page_table)
    y_s, ret_s, lru_s, lconv_s, fconv_s, k_s, v_s = _sample_trunk(
        x_sample, p_sample, state_ret, state_lru, state_lru_conv, state_ffn_conv, ck_ts, cv_ts, page_table,
        w, rel_bias, kmeans)
    return (y_p, y_s, ret_p, ret_s, lru_p, lru_s, lconv_p, lconv_s, fconv_p, fconv_s, k_p, k_s, v_p, v_s)
```
